```python
import jax, jax.numpy as jnp
from jax import lax
import numpy as np

D_MODEL = 2048
BATCH = 1
SEQ = 16384
DEPTH = 2

CHUNK = 64
NORM_EPS = 1e-6
CONV_K = 4
GDN_HEADS = 8
GDN_DK = 128
GDN_DV = 128
GDN_QK = GDN_HEADS * GDN_DK
GDN_V = GDN_HEADS * GDN_DV
ATT_HEADS = 8
ATT_DH = 128
ATT_W = ATT_HEADS * ATT_DH
BAND_CHUNKS = 9
REL_CLIP = 256
SSD_DINNER = 2 * D_MODEL
SSD_HEADDIM = 64
SSD_HEADS = SSD_DINNER // SSD_HEADDIM
SSD_GROUPS = 8
SSD_HPG = SSD_HEADS // SSD_GROUPS
SSD_DSTATE = 128
SSD_CONV = SSD_DINNER + 2 * SSD_GROUPS * SSD_DSTATE
D_FF = 4 * D_MODEL

AB_SPLITS = (GDN_QK, GDN_QK, GDN_V, GDN_V, GDN_HEADS, GDN_HEADS, ATT_W, ATT_W, ATT_W)
AB_IN = sum(AB_SPLITS)
AB_OUT = GDN_V + ATT_W
SSD_SPLITS = (SSD_DINNER, SSD_CONV, SSD_HEADS)
SSD_IN = sum(SSD_SPLITS)
N_EVEN = (DEPTH + 1) // 2
N_ODD = DEPTH // 2

kernel_name = "hybrid_gdn_bandattn_ssd_trunk"


def split_sizes(x, sizes):
    return jnp.split(x, np.cumsum(sizes)[:-1].tolist(), axis=-1)


def rms_norm(x, w):
    xf = x.astype(jnp.float32)
    y = xf * lax.rsqrt(jnp.mean(xf * xf, axis=-1, keepdims=True) + NORM_EPS)
    return (y * w.astype(jnp.float32)).astype(x.dtype)


def l2_norm(x):
    xf = x.astype(jnp.float32)
    return xf * lax.rsqrt(jnp.sum(xf * xf, axis=-1, keepdims=True) + NORM_EPS)


def causal_dwconv(x, w):
    k = w.shape[0]
    return lax.conv_general_dilated(
        x, w[:, None, :].astype(x.dtype), window_strides=(1,), padding=[(k - 1, 0)],
        dimension_numbers=('NWC', 'WIO', 'NWC'), feature_group_count=x.shape[-1])


def gated_deltanet(q, k, v, z, b_raw, a_raw, conv_w, a_log, dt_bias, norm_w):
    dtype = q.dtype
    b_, t_, _ = q.shape
    n = t_ // CHUNK
    f32 = jnp.float32
    qkv = jax.nn.silu(causal_dwconv(jnp.concatenate([q, k, v], axis=-1), conv_w))
    q, k, v = split_sizes(qkv, (GDN_QK, GDN_QK, GDN_V))
    q = l2_norm(q.reshape(b_, t_, GDN_HEADS, GDN_DK)) * (GDN_DK ** -0.5)
    k = l2_norm(k.reshape(b_, t_, GDN_HEADS, GDN_DK))
    v = v.reshape(b_, t_, GDN_HEADS, GDN_DV).astype(f32)
    beta = jax.nn.sigmoid(b_raw.astype(f32))
    g = -jnp.exp(a_log.astype(f32)) * jax.nn.softplus(a_raw.astype(f32) + dt_bias.astype(f32))

    def to_chunks(t):
        return t.reshape(b_, n, CHUNK, GDN_HEADS, -1).transpose(0, 3, 1, 2, 4)

    qc, kc, vc = to_chunks(q), to_chunks(k), to_chunks(v)
    bc = beta.reshape(b_, n, CHUNK, GDN_HEADS).transpose(0, 3, 1, 2)
    gcum = jnp.cumsum(g.reshape(b_, n, CHUNK, GDN_HEADS).transpose(0, 3, 1, 2), axis=-1)
    causal = jnp.tril(jnp.ones((CHUNK, CHUNK), dtype=bool))
    strict = jnp.tril(jnp.ones((CHUNK, CHUNK), dtype=bool), k=-1)
    decay = jnp.exp(jnp.where(causal, gcum[..., :, None] - gcum[..., None, :], -jnp.inf))
    kb = kc * bc[..., None]
    lower = jnp.where(strict, jnp.einsum('bhnid,bhnjd->bhnij', kb, kc) * decay, 0.0)
    eye = jnp.eye(CHUNK, dtype=f32)
    tmat = lax.linalg.triangular_solve(eye + lower, jnp.broadcast_to(eye, lower.shape),
                                       left_side=True, lower=True, unit_diagonal=True)
    u = jnp.einsum('bhnij,bhnje->bhnie', tmat, vc * bc[..., None])
    w = jnp.einsum('bhnij,bhnjd->bhnid', tmat, kb * jnp.exp(gcum)[..., None])
    qk = jnp.where(causal, jnp.einsum('bhnid,bhnjd->bhnij', qc, kc) * decay, 0.0)
    q_dec = qc * jnp.exp(gcum)[..., None]
    k_dec = kc * jnp.exp(gcum[..., -1:] - gcum)[..., None]
    g_last = jnp.exp(gcum[..., -1])

    def step(state, inp):
        w_n, u_n, qk_n, qd_n, kd_n, gl_n = inp
        v_new = u_n - jnp.einsum('bhld,bhde->bhle', w_n, state)
        o_n = jnp.einsum('bhld,bhde->bhle', qd_n, state) + jnp.einsum('bhls,bhse->bhle', qk_n, v_new)
        state = state * gl_n[..., None, None] + jnp.einsum('bhld,bhle->bhde', kd_n, v_new)
        return state, o_n

    xs = tuple(jnp.moveaxis(t, 2, 0) for t in (w, u, qk, q_dec, k_dec, g_last))
    s0 = jnp.zeros((b_, GDN_HEADS, GDN_DK, GDN_DV), f32)
    _, o = lax.scan(step, s0, xs)
    o = o.transpose(1, 0, 3, 2, 4).reshape(b_, t_, GDN_HEADS, GDN_DV)
    o = rms_norm(o, norm_w) * jax.nn.silu(z.reshape(b_, t_, GDN_HEADS, GDN_DV).astype(f32))
    return o.reshape(b_, t_, GDN_V).astype(dtype)


def chunk_band_attention(q, k, v, q_norm_w, k_norm_w, rel_bias):
    dtype = q.dtype
    b_, t_, _ = q.shape
    n = t_ // CHUNK
    f32 = jnp.float32

    def heads(t):
        return t.reshape(b_, n, CHUNK, ATT_HEADS, ATT_DH)

    qh = rms_norm(heads(q), q_norm_w).astype(f32) * (ATT_DH ** -0.5)
    kh = rms_norm(heads(k), k_norm_w).astype(f32)
    vh = heads(v).astype(f32)
    pad = [(0, 0), (BAND_CHUNKS - 1, 0), (0, 0), (0, 0), (0, 0)]
    kp = jnp.pad(kh, pad)
    vp = jnp.pad(vh, pad)
    scores = jnp.concatenate(
        [jnp.einsum('bnqhd,bnkhd->bhnqk', qh, kp[:, o:o + n]) for o in range(BAND_CHUNKS)],
        axis=-1)
    qi = jnp.arange(CHUNK)
    kj = jnp.arange(BAND_CHUNKS * CHUNK)
    rel = (kj[None, :] - (BAND_CHUNKS - 1) * CHUNK) - qi[:, None]
    bias = rel_bias.astype(f32)[:, jnp.clip(rel, -REL_CLIP, REL_CLIP) + REL_CLIP]
    chunk_idx = jnp.arange(n)[:, None] + jnp.arange(BAND_CHUNKS)[None, :] - (BAND_CHUNKS - 1)
    valid = jnp.repeat(chunk_idx >= 0, CHUNK, axis=1)
    scores = jnp.where(valid[None, None, :, None, :], scores + bias[None, :, None], -jnp.inf)
    p = jax.nn.softmax(scores, axis=-1)
    out = jnp.einsum('bhnqk,bnkhd->bnqhd', p[..., :CHUNK], vp[:, 0:n])
    for o in range(1, BAND_CHUNKS):
        out = out + jnp.einsum('bhnqk,bnkhd->bnqhd', p[..., o * CHUNK:(o + 1) * CHUNK], vp[:, o:o + n])
    return out.reshape(b_, t_, ATT_W).astype(dtype)


def gdn_attn_mixer(h, w_in, gdn_conv_w, gdn_a_log, gdn_dt_bias, gdn_norm_w,
                   q_norm_w, k_norm_w, rel_bias, w_out):
    proj = jnp.einsum('btd,de->bte', h, w_in)
    gq, gk, gv, gz, gb, ga, aq, ak, av = split_sizes(proj, AB_SPLITS)
    o_a = gated_deltanet(gq, gk, gv, gz, gb, ga, gdn_conv_w, gdn_a_log, gdn_dt_bias, gdn_norm_w)
    o_b = chunk_band_attention(aq, ak, av, q_norm_w, k_norm_w, rel_bias)
    return jnp.einsum('bte,ed->btd', jnp.concatenate([o_a, o_b], axis=-1), w_out)


def mamba2_ssd(h, w_in, conv_w, conv_b, dt_bias, a_log, d_skip, norm_w, w_out):
    dtype = h.dtype
    b_, t_, _ = h.shape
    n = t_ // CHUNK
    f32 = jnp.float32
    proj = jnp.einsum('btd,de->bte', h, w_in)
    z, xbc, dt_raw = split_sizes(proj, SSD_SPLITS)
    xbc = jax.nn.silu(causal_dwconv(xbc, conv_w) + conv_b)
    xs, bm, cm = split_sizes(xbc, (SSD_DINNER, SSD_GROUPS * SSD_DSTATE, SSD_GROUPS * SSD_DSTATE))
    x = xs.astype(f32).reshape(b_, n, CHUNK, SSD_GROUPS, SSD_HPG, SSD_HEADDIM)
    bm = bm.astype(f32).reshape(b_, n, CHUNK, SSD_GROUPS, SSD_DSTATE)
    cm = cm.astype(f32).reshape(b_, n, CHUNK, SSD_GROUPS, SSD_DSTATE)
    dt = jax.nn.softplus(dt_raw.astype(f32) + dt_bias.astype(f32)).reshape(b_, n, CHUNK, SSD_GROUPS, SSD_HPG)
    a = -jnp.exp(a_log.astype(f32)).reshape(SSD_GROUPS, SSD_HPG)
    acum = jnp.cumsum(dt * a, axis=2)
    xdt = x * dt[..., None]
    causal = jnp.tril(jnp.ones((CHUNK, CHUNK), dtype=bool))
    a_t = jnp.moveaxis(acum, 2, -1)
    decay = jnp.exp(jnp.where(causal, a_t[..., :, None] - a_t[..., None, :], -jnp.inf))
    cb = jnp.einsum('bnlgs,bnmgs->bnglm', cm, bm)
    y_diag = jnp.einsum('bngelm,bnmgep->bnlgep', cb[:, :, :, None] * decay, xdt)

    def step(state, inp):
        c_n, b_n, x_n, a_n = inp
        y_off = jnp.einsum('blgs,bgeps->blgep', c_n, state) * jnp.exp(a_n)[..., None]
        w_dec = jnp.exp(a_n[:, -1:] - a_n)
        state = (state * jnp.exp(a_n[:, -1])[..., None, None]
                 + jnp.einsum('blgs,blgep->bgeps', b_n, x_n * w_dec[..., None]))
        return state, y_off

    s0 = jnp.zeros((b_, SSD_GROUPS, SSD_HPG, SSD_HEADDIM, SSD_DSTATE), f32)
    xs_scan = tuple(jnp.moveaxis(t, 1, 0) for t in (cm, bm, xdt, acum))
    _, y_off = lax.scan(step, s0, xs_scan)
    y = y_diag + jnp.moveaxis(y_off, 0, 1) + x * d_skip.astype(f32).reshape(SSD_GROUPS, SSD_HPG, 1)
    y = y.reshape(b_, t_, SSD_DINNER) * jax.nn.silu(z.astype(f32))
    y = rms_norm(y.reshape(b_, t_, SSD_GROUPS, -1), norm_w.reshape(SSD_GROUPS, -1)).reshape(b_, t_, SSD_DINNER)
    return jnp.einsum('bte,ed->btd', y.astype(dtype), w_out)


def sq_relu_mlp(h, w1, w2):
    a = jax.nn.relu(jnp.einsum('btd,df->btf', h, w1))
    return jnp.einsum('btf,fd->btd', a * a, w2)


def _dt_bias(key, shape):
    dt = jnp.exp(jax.random.uniform(key, shape) * (jnp.log(0.1) - jnp.log(0.001)) + jnp.log(0.001))
    return dt + jnp.log(-jnp.expm1(-dt))


def setup_inputs(seed: int = 0) -> dict:
    key = jax.random.key(seed)
    ks = iter(jax.random.split(key, 32))
    nrm = lambda shape, s: jax.random.normal(next(ks), shape, jnp.float32) * s
    gain = lambda shape: 1.0 + 0.02 * jax.random.normal(next(ks), shape, jnp.float32)
    return {
        "x": nrm((BATCH, SEQ, D_MODEL), 1.0),
        "c": nrm((BATCH, D_MODEL), 1.0),
        "mod_w": nrm((DEPTH, D_MODEL, 6 * D_MODEL), 0.5 * D_MODEL ** -0.5),
        "mod_b": nrm((DEPTH, 6 * D_MODEL), 0.01),
        "norm_mix_w": gain((DEPTH, D_MODEL)),
        "norm_mlp_w": gain((DEPTH, D_MODEL)),
        "mlp_w1": nrm((DEPTH, D_MODEL, D_FF), D_MODEL ** -0.5),
        "mlp_w2": nrm((DEPTH, D_FF, D_MODEL), D_FF ** -0.5),
        "ab_w_in": nrm((N_EVEN, D_MODEL, AB_IN), D_MODEL ** -0.5),
        "gdn_conv_w": nrm((N_EVEN, CONV_K, 2 * GDN_QK + GDN_V), CONV_K ** -0.5),
        "gdn_a_log": jnp.log(jax.random.uniform(next(ks), (N_EVEN, GDN_HEADS), jnp.float32, 1.0, 16.0)),
        "gdn_dt_bias": _dt_bias(next(ks), (N_EVEN, GDN_HEADS)),
        "gdn_norm_w": gain((N_EVEN, GDN_DV)),
        "attn_q_norm_w": gain((N_EVEN, ATT_DH)),
        "attn_k_norm_w": gain((N_EVEN, ATT_DH)),
        "attn_rel_bias": nrm((N_EVEN, ATT_HEADS, 2 * REL_CLIP + 1), 0.1),
        "ab_w_out": nrm((N_EVEN, AB_OUT, D_MODEL), AB_OUT ** -0.5),
        "ssd_w_in": nrm((N_ODD, D_MODEL, SSD_IN), D_MODEL ** -0.5),
        "ssd_conv_w": nrm((N_ODD, CONV_K, SSD_CONV), CONV_K ** -0.5),
        "ssd_conv_b": nrm((N_ODD, SSD_CONV), 0.01),
        "ssd_dt_bias": _dt_bias(next(ks), (N_ODD, SSD_HEADS)),
        "ssd_a_log": jnp.log(jax.random.uniform(next(ks), (N_ODD, SSD_HEADS), jnp.float32, 1.0, 16.0)),
        "ssd_d": gain((N_ODD, SSD_HEADS)),
        "ssd_norm_w": gain((N_ODD, SSD_DINNER)),
        "ssd_w_out": nrm((N_ODD, SSD_DINNER, D_MODEL), SSD_DINNER ** -0.5),
    }


def reference(x, c, mod_w, mod_b, norm_mix_w, norm_mlp_w, mlp_w1, mlp_w2,
              ab_w_in, gdn_conv_w, gdn_a_log, gdn_dt_bias, gdn_norm_w,
              attn_q_norm_w, attn_k_norm_w, attn_rel_bias, ab_w_out,
              ssd_w_in, ssd_conv_w, ssd_conv_b, ssd_dt_bias, ssd_a_log, ssd_d,
              ssd_norm_w, ssd_w_out):
    c_act = jax.nn.silu(c)
    for layer in range(DEPTH):
        mod = jnp.einsum('bd,de->be', c_act, mod_w[layer]) + mod_b[layer]
        sh1, sc1, g1, sh2, sc2, g2 = [m[:, None, :] for m in jnp.split(mod, 6, axis=-1)]
        h = rms_norm(x, norm_mix_w[layer]) * (1.0 + sc1) + sh1
        i = layer // 2
        if layer % 2 == 0:
            mix = gdn_attn_mixer(h, ab_w_in[i], gdn_conv_w[i], gdn_a_log[i], gdn_dt_bias[i],
                                 gdn_norm_w[i], attn_q_norm_w[i], attn_k_norm_w[i],
                                 attn_rel_bias[i], ab_w_out[i])
        else:
            mix = mamba2_ssd(h, ssd_w_in[i], ssd_conv_w[i], ssd_conv_b[i], ssd_dt_bias[i],
                             ssd_a_log[i], ssd_d[i], ssd_norm_w[i], ssd_w_out[i])
        x = x + g1 * mix
        h = rms_norm(x, norm_mlp_w[layer]) * (1.0 + sc2) + sh2
        x = x + g2 * sq_relu_mlp(h, mlp_w1[layer], mlp_w2[layer])
    return x
```

```python
import functools

import jax
import jax.numpy as jnp
from jax import lax
from jax.experimental import pallas as pl
from jax.experimental.pallas import tpu as pltpu

F32 = jnp.float32
BF16 = jnp.bfloat16

D_MODEL = 2048
CHUNK = 64
NORM_EPS = 1e-6
CONV_K = 4
HALO = 8
GDN_HEADS = 8
GDN_DK = 128
GDN_DV = 128
GDN_QK = GDN_HEADS * GDN_DK
GDN_V = GDN_HEADS * GDN_DV
ATT_HEADS = 8
ATT_DH = 128
ATT_W = ATT_HEADS * ATT_DH
BAND_CHUNKS = 9
REL_CLIP = 256
SSD_DINNER = 2 * D_MODEL
SSD_HEADDIM = 64
SSD_HEADS = SSD_DINNER // SSD_HEADDIM
SSD_GROUPS = 8
SSD_HPG = SSD_HEADS // SSD_GROUPS
SSD_DSTATE = 128
SSD_GW = SSD_HPG * SSD_HEADDIM
D_FF = 4 * D_MODEL
LANES = 128
NEG_BIG = -1e30

VMEM_LIMIT = 56 * 1024 * 1024

_NT = (((1,), (1,)), ((), ()))
_TN = (((0,), (0,)), ((), ()))


def _dot(a, b):
    return jnp.dot(a.astype(BF16), b.astype(BF16), preferred_element_type=F32)


def _dot_nt(a, b):
    return lax.dot_general(a.astype(BF16), b.astype(BF16), _NT, preferred_element_type=F32)


def _dot_tn(a, b):
    return lax.dot_general(a.astype(BF16), b.astype(BF16), _TN, preferred_element_type=F32)


def _split3(a):
    a1 = a.astype(BF16)
    r1 = a - a1.astype(F32)
    a2 = r1.astype(BF16)
    a3 = (r1 - a2.astype(F32)).astype(BF16)
    return a1, a2, a3


def _dot_exact_rhs(a, b_bf16):
    a1, a2, a3 = _split3(a)
    out = jnp.dot(a1, b_bf16, preferred_element_type=F32)
    out += jnp.dot(a2, b_bf16, preferred_element_type=F32)
    out += jnp.dot(a3, b_bf16, preferred_element_type=F32)
    return out


def _dot_exact_lhs(a_bf16, b):
    b1, b2, b3 = _split3(b)
    out = jnp.dot(a_bf16, b1, preferred_element_type=F32)
    out += jnp.dot(a_bf16, b2, preferred_element_type=F32)
    out += jnp.dot(a_bf16, b3, preferred_element_type=F32)
    return out


def _silu(x):
    return x * jax.nn.sigmoid(x)


def _softplus(x):
    return jnp.maximum(x, 0.0) + jnp.log(1.0 + jnp.exp(-jnp.abs(x)))


def _norm_mod(x, nw, sc, sh):
    ms = jnp.mean(x * x, axis=-1, keepdims=True)
    return (x * lax.rsqrt(ms + NORM_EPS) * nw) * (1.0 + sc) + sh


def _seg_tri(n):
    r = lax.broadcasted_iota(jnp.int32, (n, n), 0)
    c = lax.broadcasted_iota(jnp.int32, (n, n), 1)
    return jnp.where((r >= c) & ((r // CHUNK) == (c // CHUNK)), 1.0, 0.0).astype(BF16)


def _params(sem):
    return pltpu.CompilerParams(dimension_semantics=sem, vmem_limit_bytes=VMEM_LIMIT)


def _mod_kernel(c_ref, w_ref, b_ref, o_ref):
    c = c_ref[...]
    o_ref[...] = jnp.dot(_silu(c), w_ref[...], preferred_element_type=F32,
                         precision=lax.Precision.HIGHEST) + b_ref[...]


def _modulation(c, mod_w, mod_b, tn=1024):
    depth, d, n = mod_w.shape
    c8 = jnp.broadcast_to(c.reshape(1, d), (8, d))
    out = pl.pallas_call(
        _mod_kernel,
        grid=(depth, n // tn),
        in_specs=[pl.BlockSpec((8, d), lambda l, j: (0, 0)),
                  pl.BlockSpec((None, d, tn), lambda l, j: (l, 0, j)),
                  pl.BlockSpec((None, 1, tn), lambda l, j: (l, 0, j))],
        out_specs=pl.BlockSpec((None, 8, tn), lambda l, j: (l, 0, j)),
        out_shape=jax.ShapeDtypeStruct((depth, 8, n), F32),
        compiler_params=_params(("arbitrary", "arbitrary")),
        name="modulation",
    )(c8, mod_w, mod_b.reshape(depth, 1, n))
    return out[:, 0, :]


def _proj_kernel(x_ref, nw_ref, sc_ref, sh_ref, w_ref, wg1_ref, wg2_ref, o_ref, g_ref, h_scr):
    @pl.when(pl.program_id(1) == 0)
    def _():
        h = _norm_mod(x_ref[...], nw_ref[...], sc_ref[...], sh_ref[...])
        hb = h.astype(BF16)
        h_scr[...] = hb
        hl = (h - hb.astype(F32)).astype(BF16)
        g = jnp.dot(hb, wg1_ref[...], preferred_element_type=F32)
        g += jnp.dot(hb, wg2_ref[...], preferred_element_type=F32)
        g += jnp.dot(hl, wg1_ref[...], preferred_element_type=F32)
        g_ref[...] = g

    o_ref[...] = jnp.dot(h_scr[...], w_ref[...], preferred_element_type=F32)


def _norm_proj(x, nw, sc, sh, w, wg, tm=512, tn=512):
    t, d = x.shape
    n = w.shape[1]
    ng = wg.shape[1]
    wg1 = wg.astype(BF16)
    wg2 = (wg - wg1.astype(F32)).astype(BF16)
    row = lambda i, j: (0, 0)
    return pl.pallas_call(
        _proj_kernel,
        grid=(t // tm, n // tn),
        in_specs=[pl.BlockSpec((tm, d), lambda i, j: (i, 0)),
                  pl.BlockSpec((1, d), row), pl.BlockSpec((1, d), row), pl.BlockSpec((1, d), row),
                  pl.BlockSpec((d, tn), lambda i, j: (0, j)),
                  pl.BlockSpec((d, ng), row), pl.BlockSpec((d, ng), row)],
        out_specs=[pl.BlockSpec((tm, tn), lambda i, j: (i, j)),
                   pl.BlockSpec((tm, ng), lambda i, j: (i, 0))],
        out_shape=[jax.ShapeDtypeStruct((t, n), F32), jax.ShapeDtypeStruct((t, ng), F32)],
        scratch_shapes=[pltpu.VMEM((tm, d), BF16)],
        compiler_params=_params(("parallel", "arbitrary")),
        name="norm_proj",
    )(x, nw, sc, sh, w.astype(BF16), wg1, wg2)


def _out_kernel(n_in, *refs):
    y_refs = refs[:n_in]
    w_refs = refs[n_in:2 * n_in]
    x_ref, g_ref, o_ref = refs[2 * n_in:]
    acc = jnp.dot(y_refs[0][...], w_refs[0][...], preferred_element_type=F32)
    for y_ref, w_ref in zip(y_refs[1:], w_refs[1:]):
        acc += jnp.dot(y_ref[...], w_ref[...], preferred_element_type=F32)
    o_ref[...] = x_ref[...] + g_ref[...] * acc


def _out_proj(ys, ws, x, gate, tm=512, tn=1024):
    t, d = x.shape
    n_in = len(ys)
    in_specs = [pl.BlockSpec((tm, y.shape[1]), lambda i, j: (i, 0)) for y in ys]
    in_specs += [pl.BlockSpec((w.shape[0], tn), lambda i, j: (0, j)) for w in ws]
    in_specs += [pl.BlockSpec((tm, tn), lambda i, j: (i, j)), pl.BlockSpec((1, tn), lambda i, j: (0, j))]
    return pl.pallas_call(
        functools.partial(_out_kernel, n_in),
        grid=(t // tm, d // tn),
        in_specs=in_specs,
        out_specs=pl.BlockSpec((tm, tn), lambda i, j: (i, j)),
        out_shape=jax.ShapeDtypeStruct((t, d), F32),
        compiler_params=_params(("parallel", "arbitrary")),
        name="out_proj",
    )(*ys, *[w.astype(BF16) for w in ws], x, gate)


def _mlp_kernel(x_ref, nw_ref, sc_ref, sh_ref, g_ref, w1_ref, w2_ref, o_ref, h_scr, acc_scr):
    f = pl.program_id(1)

    @pl.when(f == 0)
    def _():
        h_scr[...] = _norm_mod(x_ref[...], nw_ref[...], sc_ref[...], sh_ref[...]).astype(BF16)
        acc_scr[...] = jnp.zeros_like(acc_scr)

    a = jnp.maximum(jnp.dot(h_scr[...], w1_ref[...], preferred_element_type=F32), 0.0)
    acc_scr[...] += jnp.dot((a * a).astype(BF16), w2_ref[...], preferred_element_type=F32)

    @pl.when(f == pl.num_programs(1) - 1)
    def _():
        o_ref[...] = x_ref[...] + g_ref[...] * acc_scr[...]


def _mlp(x, nw, sc, sh, gate, w1, w2, tm=512, tf=512):
    t, d = x.shape
    ff = w1.shape[1]
    row = lambda i, f: (0, 0)
    return pl.pallas_call(
        _mlp_kernel,
        grid=(t // tm, ff // tf),
        in_specs=[pl.BlockSpec((tm, d), lambda i, f: (i, 0)),
                  pl.BlockSpec((1, d), row), pl.BlockSpec((1, d), row), pl.BlockSpec((1, d), row),
                  pl.BlockSpec((1, d), row),
                  pl.BlockSpec((d, tf), lambda i, f: (0, f)),
                  pl.BlockSpec((tf, d), lambda i, f: (f, 0))],
        out_specs=pl.BlockSpec((tm, d), lambda i, f: (i, 0)),
        out_shape=jax.ShapeDtypeStruct((t, d), F32),
        scratch_shapes=[pltpu.VMEM((tm, d), BF16), pltpu.VMEM((tm, d), F32)],
        compiler_params=_params(("parallel", "arbitrary")),
        name="mlp",
    )(x, nw, sc, sh, gate, w1.astype(BF16), w2.astype(BF16))


ATT_TQ = 256
ATT_KB = 3


def _attn_kernel(q_ref, k0_ref, k1_ref, k2_ref, v0_ref, v1_ref, v2_ref, bias_ref, qw_ref, kw_ref, o_ref):
    i = pl.program_id(1)

    def rms(x, w):
        return x * lax.rsqrt(jnp.mean(x * x, axis=-1, keepdims=True) + NORM_EPS) * w

    kw = kw_ref[...]
    q = rms(q_ref[...], qw_ref[...]) * (ATT_DH ** -0.5)
    k = jnp.concatenate([rms(k0_ref[...], kw), rms(k1_ref[...], kw), rms(k2_ref[...], kw)], axis=0)
    v = jnp.concatenate([v0_ref[...], v1_ref[...], v2_ref[...]], axis=0)
    s = _dot_nt(q, k) + bias_ref[...]
    col = lax.broadcasted_iota(jnp.int32, s.shape, 1)
    s = jnp.where(col >= (ATT_KB - 1 - i) * ATT_TQ, s, NEG_BIG)
    m = jnp.max(s, axis=-1, keepdims=True)
    p = jnp.exp(s - m)
    l = jnp.sum(p, axis=-1, keepdims=True)
    o_ref[...] = (_dot(p, v) / l).astype(o_ref.dtype)


def _attn_bias_table(rel_bias):
    r = jnp.arange(ATT_TQ)[:, None]
    m = jnp.arange(ATT_KB * ATT_TQ)[None, :]
    rel = m - r - (ATT_KB - 1) * ATT_TQ
    qc = r // CHUNK
    kc = m // CHUNK
    in_band = (kc >= qc) & (kc <= qc + BAND_CHUNKS - 1)
    tab = rel_bias.astype(F32)[:, jnp.clip(rel, -REL_CLIP, REL_CLIP) + REL_CLIP]
    return jnp.where(in_band[None], tab, NEG_BIG)


def _band_attention(proj, q_col, k_col, v_col, q_norm_w, k_norm_w, rel_bias):
    t = proj.shape[0]
    bias = _attn_bias_table(rel_bias)
    blk = (ATT_TQ, ATT_DH)

    def kv_spec(col, back):
        return pl.BlockSpec(blk, lambda h, i: (jnp.maximum(i - back, 0), col + h))

    return pl.pallas_call(
        _attn_kernel,
        grid=(ATT_HEADS, t // ATT_TQ),
        in_specs=[pl.BlockSpec(blk, lambda h, i: (i, q_col + h)),
                  kv_spec(k_col, 2), kv_spec(k_col, 1), kv_spec(k_col, 0),
                  kv_spec(v_col, 2), kv_spec(v_col, 1), kv_spec(v_col, 0),
                  pl.BlockSpec((None, ATT_TQ, ATT_KB * ATT_TQ), lambda h, i: (h, 0, 0)),
                  pl.BlockSpec((1, ATT_DH), lambda h, i: (0, 0)),
                  pl.BlockSpec((1, ATT_DH), lambda h, i: (0, 0))],
        out_specs=pl.BlockSpec(blk, lambda h, i: (i, h)),
        out_shape=jax.ShapeDtypeStruct((t, ATT_W), BF16),
        compiler_params=_params(("parallel", "arbitrary")),
        name="band_attention",
    )(proj, proj, proj, proj, proj, proj, proj, bias,
      q_norm_w.reshape(1, ATT_DH), k_norm_w.reshape(1, ATT_DH))


GDN_TB = 512
GDN_NC = GDN_TB // CHUNK


def _causal_conv_silu(cur_ref, prev_ref, w_ref, scr, first_block):
    rows = cur_ref.shape[0]
    prev = prev_ref[...]
    scr[0:HALO, :] = jnp.where(first_block, jnp.zeros_like(prev), prev)
    scr[HALO:HALO + rows, :] = cur_ref[...]
    w = w_ref[...]
    acc = scr[pl.ds(HALO - (CONV_K - 1), rows), :] * w[0:1, :]
    for j in range(1, CONV_K):
        acc += scr[pl.ds(HALO - (CONV_K - 1) + j, rows), :] * w[j:j + 1, :]
    return acc


def _gdn_kernel(qc_ref, qp_ref, kc_ref, kp_ref, vc_ref, vp_ref, wq_ref, wk_ref, wv_ref,
                gate_ref, alog_ref, dtb_ref, z_ref, nw_ref, o_ref,
                cq_scr, ck_scr, cv_scr, beta_scr, gcum_scr, gt_scr,
                wq_scr, u_scr, qk_scr, kd_scr, gl_scr, s_scr):
    i = pl.program_id(0)
    h = pl.program_id(1)
    tb = GDN_TB
    first = i == 0

    @pl.when(first & (h == 0))
    def _():
        s_scr[...] = jnp.zeros_like(s_scr)

    @pl.when(h == 0)
    def _():
        gate = gate_ref[...]
        beta_scr[...] = jax.nn.sigmoid(gate)
        g = -jnp.exp(alog_ref[...]) * _softplus(gate + dtb_ref[...])
        gcum = _dot_exact_lhs(_seg_tri(tb), g)
        gcum_scr[...] = gcum
        for s in range(tb // LANES):
            gt_scr[s] = gcum[s * LANES:(s + 1) * LANES, :].T

    lane = lax.broadcasted_iota(jnp.int32, (1, LANES), 1)
    beta = jnp.sum(jnp.where(lane == h, beta_scr[...], 0.0), axis=1, keepdims=True)
    gc = jnp.sum(jnp.where(lane == GDN_HEADS + h, gcum_scr[...], 0.0), axis=1, keepdims=True)

    def l2n(x):
        return x * lax.rsqrt(jnp.sum(x * x, axis=-1, keepdims=True) + NORM_EPS)

    q = l2n(_silu(_causal_conv_silu(qc_ref, qp_ref, wq_ref, cq_scr, first))) * (GDN_DK ** -0.5)
    k = l2n(_silu(_causal_conv_silu(kc_ref, kp_ref, wk_ref, ck_scr, first)))
    v = _silu(_causal_conv_silu(vc_ref, vp_ref, wv_ref, cv_scr, first))

    eg = jnp.exp(gc)
    kb = k * beta
    rhs_all = jnp.concatenate([v * beta, kb * eg], axis=1)
    qd = q * eg

    r = lax.broadcasted_iota(jnp.int32, (CHUNK, CHUNK), 0)
    c = lax.broadcasted_iota(jnp.int32, (CHUNK, CHUNK), 1)
    causal = r >= c
    strict = r > c
    eye = jnp.where(r == c, 1.0, 0.0)

    for n in range(GDN_NC):
        rows = slice(n * CHUNK, (n + 1) * CHUNK)
        k_n = k[rows]
        gc_n = gc[rows]
        grow = gt_scr[n // 2, pl.ds(GDN_HEADS + h, 1), :]
        grow = grow[:, (n % 2) * CHUNK:(n % 2 + 1) * CHUNK]
        g_last = gc_n[CHUNK - 1:CHUNK, :]
        diff = jnp.where(causal, gc_n - grow, 0.0)
        decay = jnp.where(causal, jnp.exp(diff), 0.0)
        ab = _dot_nt(jnp.concatenate([kb[rows], q[rows]], axis=0), k_n)
        low = jnp.where(strict, ab[:CHUNK] * decay, 0.0)
        qk = ab[CHUNK:] * decay
        p = eye - low
        x = _dot(low, low)
        for step in range(5):
            p = p + _dot(p, x)
            if step < 4:
                x = _dot(x, x)
        uw = _dot(p, rhs_all[rows])
        u_scr[h, rows, :] = uw[:, :GDN_DV]
        wq_scr[h, n, 0:CHUNK, :] = uw[:, GDN_DV:].astype(BF16)
        wq_scr[h, n, CHUNK:2 * CHUNK, :] = qd[rows].astype(BF16)
        qk_scr[h, rows, :] = qk.astype(BF16)
        kd_scr[h, rows, :] = (k_n * jnp.exp(g_last - gc_n)).astype(BF16)
        gl_scr[h, n] = jnp.broadcast_to(jnp.exp(g_last), (8, LANES))

    @pl.when(h == GDN_HEADS - 1)
    def _():
        nw = nw_ref[...]

        def chunk_step(n, carry):
            r0 = pl.multiple_of(n * CHUNK, CHUNK)
            for hh in range(GDN_HEADS):
                state = s_scr[hh]
                ws = jnp.dot(wq_scr[hh, n], state.astype(BF16), preferred_element_type=F32)
                v_new = u_scr[hh, pl.ds(r0, CHUNK), :] - ws[:CHUNK]
                vb = v_new.astype(BF16)
                o = ws[CHUNK:] + jnp.dot(qk_scr[hh, pl.ds(r0, CHUNK), :], vb, preferred_element_type=F32)
                upd = lax.dot_general(kd_scr[hh, pl.ds(r0, CHUNK), :], vb, _TN, preferred_element_type=F32)
                s_scr[hh] = state * gl_scr[hh, n][0:1, :] + upd
                on = o * lax.rsqrt(jnp.mean(o * o, axis=-1, keepdims=True) + NORM_EPS) * nw
                zz = z_ref[pl.ds(r0, CHUNK), hh * GDN_DV:(hh + 1) * GDN_DV]
                o_ref[pl.ds(r0, CHUNK), hh * GDN_DV:(hh + 1) * GDN_DV] = (on * _silu(zz)).astype(o_ref.dtype)
            return carry

        lax.fori_loop(0, GDN_NC, chunk_step, 0)


def _gated_deltanet(proj, gates, conv_w, a_log, dt_bias, norm_w):
    t = proj.shape[0]
    tb = GDN_TB
    nh = GDN_HEADS
    pad = lambda a: jnp.zeros((1, LANES), F32).at[0, nh:2 * nh].set(a.astype(F32))
    cur = lambda col: pl.BlockSpec((tb, LANES), lambda i, h: (i, col + h))
    prev = lambda col: pl.BlockSpec((HALO, LANES), lambda i, h: (jnp.maximum(i * (tb // HALO) - 1, 0), col + h))
    cw = lambda col: pl.BlockSpec((CONV_K, LANES), lambda i, h: (0, col + h))
    row = pl.BlockSpec((1, LANES), lambda i, h: (0, 0))
    return pl.pallas_call(
        _gdn_kernel,
        grid=(t // tb, nh),
        in_specs=[cur(0), prev(0), cur(nh), prev(nh), cur(2 * nh), prev(2 * nh),
                  cw(0), cw(nh), cw(2 * nh),
                  pl.BlockSpec((tb, LANES), lambda i, h: (i, 0)), row, row,
                  pl.BlockSpec((tb, GDN_V), lambda i, h: (i, 3)), row],
        out_specs=pl.BlockSpec((tb, GDN_V), lambda i, h: (i, 0)),
        out_shape=jax.ShapeDtypeStruct((t, GDN_V), BF16),
        scratch_shapes=[pltpu.VMEM((tb + HALO, LANES), F32)] * 3 + [
            pltpu.VMEM((tb, LANES), F32), pltpu.VMEM((tb, LANES), F32),
            pltpu.VMEM((tb // LANES, LANES, LANES), F32),
            pltpu.VMEM((nh, GDN_NC, 2 * CHUNK, GDN_DK), BF16),
            pltpu.VMEM((nh, tb, GDN_DV), F32),
            pltpu.VMEM((nh, tb, CHUNK), BF16),
            pltpu.VMEM((nh, tb, GDN_DK), BF16),
            pltpu.VMEM((nh, GDN_NC, 8, LANES), F32),
            pltpu.VMEM((nh, GDN_DK, GDN_DV), F32)],
        compiler_params=_params(("arbitrary", "arbitrary")),
        name="gated_deltanet",
    )(proj, proj, proj, proj, proj, proj, conv_w, conv_w, conv_w,
      gates, pad(a_log), pad(dt_bias), proj, norm_w.reshape(1, GDN_DV))


SSD_TB = 256
SSD_NC = SSD_TB // CHUNK


def _ssd_kernel(xc_ref, xp_ref, bc_ref, bp_ref, cc_ref, cp_ref, wx_ref, wb_ref, wc_ref,
                bx_ref, bb_ref, bcn_ref, dt_ref, dtb_ref, alog_ref, dsk_ref, z_ref, nw_ref, o_ref,
                cx_scr, cb_scr, cc_scr, dt_scr, ac_scr, st_scr):
    i = pl.program_id(0)
    g = pl.program_id(1)
    tb = SSD_TB
    first = i == 0

    @pl.when(first)
    def _():
        st_scr[g] = jnp.zeros(st_scr.shape[1:], F32)

    @pl.when(g == 0)
    def _():
        dt = _softplus(dt_ref[...] + dtb_ref[...])
        dt_scr[...] = dt
        ac_scr[...] = _dot_exact_lhs(_seg_tri(tb), dt * (-jnp.exp(alog_ref[...])))

    shift = (LANES - g * SSD_HPG) % LANES
    dt_g = pltpu.roll(dt_scr[...], shift, 1)
    ac_g = pltpu.roll(ac_scr[...], shift, 1)
    ac_t = [ac_g[s * LANES:(s + 1) * LANES, :].T for s in range(tb // LANES)]

    x = _silu(_causal_conv_silu(xc_ref, xp_ref, wx_ref, cx_scr, first) + bx_ref[...])
    bm = _silu(_causal_conv_silu(bc_ref, bp_ref, wb_ref, cb_scr, first) + bb_ref[...])
    cm = _silu(_causal_conv_silu(cc_ref, cp_ref, wc_ref, cc_scr, first) + bcn_ref[...])

    r = lax.broadcasted_iota(jnp.int32, (CHUNK, CHUNK), 0)
    c = lax.broadcasted_iota(jnp.int32, (CHUNK, CHUNK), 1)
    causal = r >= c
    dsk = dsk_ref[...]
    nw = nw_ref[...]
    p = SSD_HEADDIM

    for n in range(SSD_NC):
        rows = slice(n * CHUNK, (n + 1) * CHUNK)
        x_n, b_n, c_n = x[rows], bm[rows], cm[rows]
        state = st_scr[g]
        cb = _dot_nt(c_n, b_n)
        y_off = _dot(c_n, state)
        ys, xws, decs = [], [], []
        for e in range(SSD_HPG):
            dt_col = dt_g[rows, e:e + 1]
            a_col = ac_g[rows, e:e + 1]
            a_row = ac_t[n // 2][e:e + 1, (n % 2) * CHUNK:(n % 2 + 1) * CHUNK]
            a_last = a_col[CHUNK - 1:CHUNK, :]
            decay = jnp.where(causal, jnp.exp(jnp.where(causal, a_col - a_row, 0.0)), 0.0)
            x_e = x_n[:, e * p:(e + 1) * p]
            xdt = x_e * dt_col
            y_e = _dot(cb * decay, xdt) + y_off[:, e * p:(e + 1) * p] * jnp.exp(a_col)
            ys.append(y_e)
            xws.append(xdt * jnp.exp(a_last - a_col))
            decs.append(jnp.broadcast_to(jnp.exp(a_last), (1, p)))
        y = jnp.concatenate(ys, axis=1) + x_n * dsk
        st_scr[g] = state * jnp.concatenate(decs, axis=1) + _dot_tn(b_n, jnp.concatenate(xws, axis=1))
        y = y * _silu(z_ref[rows, :])
        y = y * lax.rsqrt(jnp.mean(y * y, axis=-1, keepdims=True) + NORM_EPS) * nw
        o_ref[rows, :] = y.astype(o_ref.dtype)


def _mamba2_ssd(proj, dt_raw, conv_w, conv_b, dt_bias, a_log, d_skip, norm_w):
    t = proj.shape[0]
    tb = SSD_TB
    gw = SSD_GW
    ng = SSD_GROUPS
    x0 = SSD_DINNER // gw
    b0 = 2 * SSD_DINNER // LANES
    c0 = b0 + ng
    pad = lambda a: jnp.zeros((1, LANES), F32).at[0, :SSD_HEADS].set(a.astype(F32))
    prev_idx = lambda i: jnp.maximum(i * (tb // HALO) - 1, 0)
    cbias = conv_b.reshape(1, -1)
    dsk = jnp.repeat(d_skip.astype(F32), SSD_HEADDIM).reshape(1, SSD_DINNER)
    return pl.pallas_call(
        _ssd_kernel,
        grid=(t // tb, ng),
        in_specs=[pl.BlockSpec((tb, gw), lambda i, g: (i, x0 + g)),
                  pl.BlockSpec((HALO, gw), lambda i, g: (prev_idx(i), x0 + g)),
                  pl.BlockSpec((tb, LANES), lambda i, g: (i, b0 + g)),
                  pl.BlockSpec((HALO, LANES), lambda i, g: (prev_idx(i), b0 + g)),
                  pl.BlockSpec((tb, LANES), lambda i, g: (i, c0 + g)),
                  pl.BlockSpec((HALO, LANES), lambda i, g: (prev_idx(i), c0 + g)),
                  pl.BlockSpec((CONV_K, gw), lambda i, g: (0, g)),
                  pl.BlockSpec((CONV_K, LANES), lambda i, g: (0, SSD_DINNER // LANES + g)),
                  pl.BlockSpec((CONV_K, LANES), lambda i, g: (0, SSD_DINNER // LANES + ng + g)),
                  pl.BlockSpec((1, gw), lambda i, g: (0, g)),
                  pl.BlockSpec((1, LANES), lambda i, g: (0, SSD_DINNER // LANES + g)),
                  pl.BlockSpec((1, LANES), lambda i, g: (0, SSD_DINNER // LANES + ng + g)),
                  pl.BlockSpec((tb, LANES), lambda i, g: (i, 0)),
                  pl.BlockSpec((1, LANES), lambda i, g: (0, 0)),
                  pl.BlockSpec((1, LANES), lambda i, g: (0, 0)),
                  pl.BlockSpec((1, gw), lambda i, g: (0, g)),
                  pl.BlockSpec((tb, gw), lambda i, g: (i, g)),
                  pl.BlockSpec((1, gw), lambda i, g: (0, g))],
        out_specs=pl.BlockSpec((tb, gw), lambda i, g: (i, g)),
        out_shape=jax.ShapeDtypeStruct((t, SSD_DINNER), BF16),
        scratch_shapes=[pltpu.VMEM((tb + HALO, gw), F32), pltpu.VMEM((tb + HALO, LANES), F32),
                        pltpu.VMEM((tb + HALO, LANES), F32),
                        pltpu.VMEM((tb, LANES), F32), pltpu.VMEM((tb, LANES), F32),
                        pltpu.VMEM((ng, SSD_DSTATE, gw), F32)],
        compiler_params=_params(("arbitrary", "arbitrary")),
        name="mamba2_ssd",
    )(proj, proj, proj, proj, proj, proj, conv_w, conv_w, conv_w, cbias, cbias, cbias,
      dt_raw, pad(dt_bias), pad(a_log), dsk, proj, norm_w.reshape(1, SSD_DINNER))


def _layer_mods(mod, layer):
    d = D_MODEL
    return [mod[layer, k * d:(k + 1) * d].reshape(1, d) for k in range(6)]


def kernel(x, c, mod_w, mod_b, norm_mix_w, norm_mlp_w, mlp_w1, mlp_w2, ab_w_in, gdn_conv_w, gdn_a_log,
           gdn_dt_bias, gdn_norm_w, attn_q_norm_w, attn_k_norm_w, attn_rel_bias, ab_w_out, ssd_w_in,
           ssd_conv_w, ssd_conv_b, ssd_dt_bias, ssd_a_log, ssd_d, ssd_norm_w, ssd_w_out):
    b, t, d = x.shape
    assert b == 1 and d == D_MODEL
    xs = x.reshape(t, d)
    mod = _modulation(c, mod_w, mod_b)

    sh1, sc1, g1, sh2, sc2, g2 = _layer_mods(mod, 0)
    w_in = ab_w_in[0]
    gate_lo = 2 * GDN_QK + 2 * GDN_V
    gate_hi = gate_lo + 2 * GDN_HEADS
    w_main = jnp.concatenate([w_in[:, :gate_lo], w_in[:, gate_hi:]], axis=1)
    w_gate = jnp.pad(w_in[:, gate_lo:gate_hi], ((0, 0), (0, LANES - 2 * GDN_HEADS)))
    proj, gates = _norm_proj(xs, norm_mix_w[0].reshape(1, d), sc1, sh1, w_main, w_gate)
    o_a = _gated_deltanet(proj, gates, gdn_conv_w[0], gdn_a_log[0], gdn_dt_bias[0], gdn_norm_w[0])
    a0 = gate_lo // LANES
    o_b = _band_attention(proj, a0, a0 + ATT_HEADS, a0 + 2 * ATT_HEADS,
                          attn_q_norm_w[0], attn_k_norm_w[0], attn_rel_bias[0])
    xs = _out_proj([o_a, o_b], [ab_w_out[0][:GDN_V], ab_w_out[0][GDN_V:]], xs, g1)
    xs = _mlp(xs, norm_mlp_w[0].reshape(1, d), sc2, sh2, g2, mlp_w1[0], mlp_w2[0])

    sh1, sc1, g1, sh2, sc2, g2 = _layer_mods(mod, 1)
    w_in = ssd_w_in[0]
    n_main = 2 * SSD_DINNER + 2 * SSD_GROUPS * SSD_DSTATE
    w_dt = jnp.pad(w_in[:, n_main:], ((0, 0), (0, LANES - SSD_HEADS)))
    proj, dt_raw = _norm_proj(xs, norm_mix_w[1].reshape(1, d), sc1, sh1, w_in[:, :n_main], w_dt)
    y = _mamba2_ssd(proj, dt_raw, ssd_conv_w[0], ssd_conv_b[0], ssd_dt_bias[0], ssd_a_log[0],
                    ssd_d[0], ssd_norm_w[0])
    xs = _out_proj([y], [ssd_w_out[0]], xs, g1)
    xs = _mlp(xs, norm_mlp_w[1].reshape(1, d), sc2, sh2, g2, mlp_w1[1], mlp_w2[1])
    return xs.reshape(b, t, d)
```

```python
import functools

import jax
import jax.numpy as jnp
from jax import lax
from jax.experimental import pallas as pl
from jax.experimental.pallas import tpu as pltpu

F32 = jnp.float32
BF16 = jnp.bfloat16

D_MODEL = 2048
CHUNK = 64
NORM_EPS = 1e-6
CONV_K = 4
HALO = 8
GDN_HEADS = 8
GDN_DK = 128
GDN_DV = 128
GDN_QK = GDN_HEADS * GDN_DK
GDN_V = GDN_HEADS * GDN_DV
ATT_HEADS = 8
ATT_DH = 128
ATT_W = ATT_HEADS * ATT_DH
BAND_CHUNKS = 9
REL_CLIP = 256
SSD_DINNER = 2 * D_MODEL
SSD_HEADDIM = 64
SSD_HEADS = SSD_DINNER // SSD_HEADDIM
SSD_GROUPS = 8
SSD_HPG = SSD_HEADS // SSD_GROUPS
SSD_DSTATE = 128
SSD_GW = SSD_HPG * SSD_HEADDIM
D_FF = 4 * D_MODEL
LANES = 128
NEG_BIG = -1e30

VMEM_LIMIT = 56 * 1024 * 1024

_NT = (((1,), (1,)), ((), ()))
_TN = (((0,), (0,)), ((), ()))


def _dot(a, b):
    return jnp.dot(a.astype(BF16), b.astype(BF16), preferred_element_type=F32)


def _dot_nt(a, b):
    return lax.dot_general(a.astype(BF16), b.astype(BF16), _NT, preferred_element_type=F32)


def _dot_tn(a, b):
    return lax.dot_general(a.astype(BF16), b.astype(BF16), _TN, preferred_element_type=F32)


def _split3(a):
    a1 = a.astype(BF16)
    r1 = a - a1.astype(F32)
    a2 = r1.astype(BF16)
    a3 = (r1 - a2.astype(F32)).astype(BF16)
    return a1, a2, a3


def _dot_exact_rhs(a, b_bf16):
    a1, a2, a3 = _split3(a)
    out = jnp.dot(a1, b_bf16, preferred_element_type=F32)
    out += jnp.dot(a2, b_bf16, preferred_element_type=F32)
    out += jnp.dot(a3, b_bf16, preferred_element_type=F32)
    return out


def _dot_exact_lhs(a_bf16, b):
    b1, b2, b3 = _split3(b)
    out = jnp.dot(a_bf16, b1, preferred_element_type=F32)
    out += jnp.dot(a_bf16, b2, preferred_element_type=F32)
    out += jnp.dot(a_bf16, b3, preferred_element_type=F32)
    return out


def _silu(x):
    return x * jax.nn.sigmoid(x)


def _softplus(x):
    return jnp.maximum(x, 0.0) + jnp.log(1.0 + jnp.exp(-jnp.abs(x)))


def _norm_mod(x, nw, sc, sh):
    ms = jnp.mean(x * x, axis=-1, keepdims=True)
    return (x * lax.rsqrt(ms + NORM_EPS) * nw) * (1.0 + sc) + sh


def _seg_tri(n):
    r = lax.broadcasted_iota(jnp.int32, (n, n), 0)
    c = lax.broadcasted_iota(jnp.int32, (n, n), 1)
    return jnp.where((r >= c) & ((r // CHUNK) == (c // CHUNK)), 1.0, 0.0).astype(BF16)


def _params(sem):
    return pltpu.CompilerParams(dimension_semantics=sem, vmem_limit_bytes=VMEM_LIMIT)


def _mod_kernel(c_ref, w_ref, b_ref, o_ref):
    c = c_ref[...]
    o_ref[...] = jnp.dot(_silu(c), w_ref[...], preferred_element_type=F32,
                         precision=lax.Precision.HIGHEST) + b_ref[...]


def _modulation(c, mod_w, mod_b, tn=1024):
    depth, d, n = mod_w.shape
    c8 = jnp.broadcast_to(c.reshape(1, d), (8, d))
    out = pl.pallas_call(
        _mod_kernel,
        grid=(depth, n // tn),
        in_specs=[pl.BlockSpec((8, d), lambda l, j: (0, 0)),
                  pl.BlockSpec((None, d, tn), lambda l, j: (l, 0, j)),
                  pl.BlockSpec((None, 1, tn), lambda l, j: (l, 0, j))],
        out_specs=pl.BlockSpec((None, 8, tn), lambda l, j: (l, 0, j)),
        out_shape=jax.ShapeDtypeStruct((depth, 8, n), F32),
        compiler_params=_params(("arbitrary", "arbitrary")),
        name="modulation",
    )(c8, mod_w, mod_b.reshape(depth, 1, n))
    return out[:, 0, :]


def _proj_kernel(x_ref, nw_ref, sc_ref, sh_ref, w_ref, wg_ref, o_ref, g_ref):
    h = _norm_mod(x_ref[...], nw_ref[...], sc_ref[...], sh_ref[...])
    hb = h.astype(BF16)
    o_ref[...] = jnp.dot(hb, w_ref[...], preferred_element_type=F32)
    ng = g_ref.shape[-1]

    @pl.when(pl.program_id(0) == 0)
    def _():
        hl = (h - hb.astype(F32)).astype(BF16)
        r = jnp.dot(hb, wg_ref[...], preferred_element_type=F32)
        g_ref[...] = r[:, :ng] + r[:, ng:] + jnp.dot(hl, wg_ref[:, :ng], preferred_element_type=F32)

    @pl.when(pl.program_id(0) != 0)
    def _():
        g_ref[...] = jnp.zeros_like(g_ref)


def _norm_proj(x, nw, sc, sh, w, wg, tm, n_split=2):
    t, d = x.shape
    n = w.shape[1]
    ng = wg.shape[1]
    tn = n // n_split
    wg1 = wg.astype(BF16)
    wg2 = (wg - wg1.astype(F32)).astype(BF16)
    row = lambda s, i: (0, 0)
    out, gates = pl.pallas_call(
        _proj_kernel,
        grid=(n_split, t // tm),
        in_specs=[pl.BlockSpec((tm, d), lambda s, i: (i, 0)),
                  pl.BlockSpec((1, d), row), pl.BlockSpec((1, d), row), pl.BlockSpec((1, d), row),
                  pl.BlockSpec((d, tn), lambda s, i: (0, s), pipeline_mode=pl.Buffered(1)),
                  pl.BlockSpec((d, 2 * ng), row, pipeline_mode=pl.Buffered(1))],
        out_specs=[pl.BlockSpec((tm, tn), lambda s, i: (i, s)),
                   pl.BlockSpec((None, tm, ng), lambda s, i: (s, i, 0))],
        out_shape=[jax.ShapeDtypeStruct((t, n), F32), jax.ShapeDtypeStruct((n_split, t, ng), F32)],
        compiler_params=_params(("arbitrary", "arbitrary")),
        name="norm_proj",
    )(x, nw, sc, sh, w.astype(BF16), jnp.concatenate([wg1, wg2], axis=1))
    return out, gates[0]


def _out_kernel(n_in, *refs):
    y_refs = refs[:n_in]
    w_refs = refs[n_in:2 * n_in]
    x_ref, g_ref, o_ref = refs[2 * n_in:]
    acc = jnp.dot(y_refs[0][...], w_refs[0][...], preferred_element_type=F32)
    for y_ref, w_ref in zip(y_refs[1:], w_refs[1:]):
        acc += jnp.dot(y_ref[...], w_ref[...], preferred_element_type=F32)
    o_ref[...] = x_ref[...] + g_ref[...] * acc


def _out_proj(ys, ws, x, gate, tm=512, tn=1024):
    t, d = x.shape
    n_in = len(ys)
    in_specs = [pl.BlockSpec((tm, y.shape[1]), lambda i, j: (i, 0)) for y in ys]
    in_specs += [pl.BlockSpec((w.shape[0], tn), lambda i, j: (0, j)) for w in ws]
    in_specs += [pl.BlockSpec((tm, tn), lambda i, j: (i, j)), pl.BlockSpec((1, tn), lambda i, j: (0, j))]
    return pl.pallas_call(
        functools.partial(_out_kernel, n_in),
        grid=(t // tm, d // tn),
        in_specs=in_specs,
        out_specs=pl.BlockSpec((tm, tn), lambda i, j: (i, j)),
        out_shape=jax.ShapeDtypeStruct((t, d), F32),
        compiler_params=_params(("parallel", "arbitrary")),
        name="out_proj",
    )(*ys, *[w.astype(BF16) for w in ws], x, gate)


def _mlp_kernel(x_ref, nw_ref, sc_ref, sh_ref, g_ref, w1_ref, w2_ref, o_ref, h_scr):
    f = pl.program_id(1)

    @pl.when(f == 0)
    def _():
        h_scr[...] = _norm_mod(x_ref[...], nw_ref[...], sc_ref[...], sh_ref[...]).astype(BF16)
        o_ref[...] = jnp.zeros_like(o_ref)

    a = jnp.maximum(jnp.dot(h_scr[...], w1_ref[...], preferred_element_type=F32), 0.0)
    o_ref[...] += jnp.dot((a * a).astype(BF16), w2_ref[...], preferred_element_type=F32)

    @pl.when(f == pl.num_programs(1) - 1)
    def _():
        o_ref[...] = x_ref[...] + g_ref[...] * o_ref[...]


def _mlp(x, nw, sc, sh, gate, w1, w2, tm=1024, tf=512):
    t, d = x.shape
    ff = w1.shape[1]
    row = lambda i, f: (0, 0)
    return pl.pallas_call(
        _mlp_kernel,
        grid=(t // tm, ff // tf),
        in_specs=[pl.BlockSpec((tm, d), lambda i, f: (i, 0), pipeline_mode=pl.Buffered(1)),
                  pl.BlockSpec((1, d), row), pl.BlockSpec((1, d), row), pl.BlockSpec((1, d), row),
                  pl.BlockSpec((1, d), row),
                  pl.BlockSpec((d, tf), lambda i, f: (0, f)),
                  pl.BlockSpec((tf, d), lambda i, f: (f, 0))],
        out_specs=pl.BlockSpec((tm, d), lambda i, f: (i, 0)),
        out_shape=jax.ShapeDtypeStruct((t, d), F32),
        scratch_shapes=[pltpu.VMEM((tm, d), BF16)],
        compiler_params=_params(("parallel", "arbitrary")),
        name="mlp",
    )(x, nw, sc, sh, gate, w1.astype(BF16), w2.astype(BF16))


ATT_TQ = 256
ATT_KB = 3


def _attn_kernel(q_ref, k0_ref, k1_ref, k2_ref, v0_ref, v1_ref, v2_ref, bias_ref, qw_ref, kw_ref, o_ref):
    i = pl.program_id(1)

    def rms(x, w):
        return x * lax.rsqrt(jnp.mean(x * x, axis=-1, keepdims=True) + NORM_EPS) * w

    kw = kw_ref[...]
    q = rms(q_ref[...], qw_ref[...]) * (ATT_DH ** -0.5)
    k = jnp.concatenate([rms(k0_ref[...], kw), rms(k1_ref[...], kw), rms(k2_ref[...], kw)], axis=0)
    v = jnp.concatenate([v0_ref[...], v1_ref[...], v2_ref[...]], axis=0)
    s = _dot_nt(q, k) + bias_ref[...]
    col = lax.broadcasted_iota(jnp.int32, s.shape, 1)
    s = jnp.where(col >= (ATT_KB - 1 - i) * ATT_TQ, s, NEG_BIG)
    m = jnp.max(s, axis=-1, keepdims=True)
    p = jnp.exp(s - m)
    l = jnp.sum(p, axis=-1, keepdims=True)
    o_ref[...] = (_dot(p, v) / l).astype(o_ref.dtype)


def _attn_bias_table(rel_bias):
    nk = ATT_KB * ATT_TQ
    span = ATT_TQ + nk
    dist = jnp.arange(span) - (ATT_TQ - 1) - (ATT_KB - 1) * ATT_TQ
    ext = rel_bias.astype(F32)[:, jnp.clip(dist, -REL_CLIP, REL_CLIP) + REL_CLIP]
    flat = jnp.tile(ext, (1, ATT_TQ))[:, :ATT_TQ * (span - 1)]
    tab = flat.reshape(-1, ATT_TQ, span - 1)[:, :, ATT_TQ - 1:ATT_TQ - 1 + nk]
    r = jnp.arange(ATT_TQ)[:, None]
    m = jnp.arange(nk)[None, :]
    qc = r // CHUNK
    kc = m // CHUNK
    in_band = (kc >= qc) & (kc <= qc + BAND_CHUNKS - 1)
    return jnp.where(in_band[None], tab, NEG_BIG)


def _band_attention(proj, q_col, k_col, v_col, q_norm_w, k_norm_w, rel_bias):
    t = proj.shape[0]
    bias = _attn_bias_table(rel_bias)
    blk = (ATT_TQ, ATT_DH)

    def kv_spec(col, back):
        return pl.BlockSpec(blk, lambda h, i: (jnp.maximum(i - back, 0), col + h))

    return pl.pallas_call(
        _attn_kernel,
        grid=(ATT_HEADS, t // ATT_TQ),
        in_specs=[pl.BlockSpec(blk, lambda h, i: (i, q_col + h)),
                  kv_spec(k_col, 2), kv_spec(k_col, 1), kv_spec(k_col, 0),
                  kv_spec(v_col, 2), kv_spec(v_col, 1), kv_spec(v_col, 0),
                  pl.BlockSpec((None, ATT_TQ, ATT_KB * ATT_TQ), lambda h, i: (h, 0, 0)),
                  pl.BlockSpec((1, ATT_DH), lambda h, i: (0, 0)),
                  pl.BlockSpec((1, ATT_DH), lambda h, i: (0, 0))],
        out_specs=pl.BlockSpec(blk, lambda h, i: (i, h)),
        out_shape=jax.ShapeDtypeStruct((t, ATT_W), BF16),
        compiler_params=_params(("parallel", "arbitrary")),
        name="band_attention",
    )(proj, proj, proj, proj, proj, proj, proj, bias,
      q_norm_w.reshape(1, ATT_DH), k_norm_w.reshape(1, ATT_DH))


GDN_TB = 512
GDN_NC = GDN_TB // CHUNK


def _causal_conv(cur_ref, prev_ref, w_ref, scr, first_block):
    rows = cur_ref.shape[0]
    prev = prev_ref[...]
    scr[0:HALO, :] = jnp.where(first_block, jnp.zeros_like(prev), prev)
    scr[HALO:HALO + rows, :] = cur_ref[...]
    w = w_ref[...]
    acc = scr[pl.ds(HALO - (CONV_K - 1), rows), :] * w[0:1, :]
    for j in range(1, CONV_K):
        acc += scr[pl.ds(HALO - (CONV_K - 1) + j, rows), :] * w[j:j + 1, :]
    return acc


def _gdn_kernel(qc_ref, qp_ref, kc_ref, kp_ref, vc_ref, vp_ref, wq_ref, wk_ref, wv_ref,
                gate_ref, alog_ref, dtb_ref, z_ref, nw_ref, o_ref,
                cq_scr, ck_scr, cv_scr, beta_scr, gcum_scr, gt_scr,
                wq_scr, u_scr, qk_scr, kd_scr, gl_scr, s_scr):
    i = pl.program_id(0)
    h = pl.program_id(1)
    tb = GDN_TB
    first = i == 0

    @pl.when(first & (h == 0))
    def _():
        s_scr[...] = jnp.zeros_like(s_scr)

    @pl.when(h == 0)
    def _():
        gate = gate_ref[...]
        beta_scr[...] = jax.nn.sigmoid(gate)
        g = -jnp.exp(alog_ref[...]) * _softplus(gate + dtb_ref[...])
        gcum = _dot_exact_lhs(_seg_tri(tb), g)
        gcum_scr[...] = gcum
        for s in range(tb // LANES):
            gt_scr[s] = gcum[s * LANES:(s + 1) * LANES, :].T

    lane = lax.broadcasted_iota(jnp.int32, (1, LANES), 1)
    beta = jnp.sum(jnp.where(lane == h, beta_scr[...], 0.0), axis=1, keepdims=True)
    gc = jnp.sum(jnp.where(lane == GDN_HEADS + h, gcum_scr[...], 0.0), axis=1, keepdims=True)

    def l2n(x):
        return x * lax.rsqrt(jnp.sum(x * x, axis=-1, keepdims=True) + NORM_EPS)

    q = l2n(_silu(_causal_conv(qc_ref, qp_ref, wq_ref, cq_scr, first))) * (GDN_DK ** -0.5)
    k = l2n(_silu(_causal_conv(kc_ref, kp_ref, wk_ref, ck_scr, first)))
    v = _silu(_causal_conv(vc_ref, vp_ref, wv_ref, cv_scr, first))

    eg = jnp.exp(gc)
    kb = k * beta
    rhs_all = jnp.concatenate([v * beta, kb * eg], axis=1)
    qd = q * eg

    r = lax.broadcasted_iota(jnp.int32, (CHUNK, CHUNK), 0)
    c = lax.broadcasted_iota(jnp.int32, (CHUNK, CHUNK), 1)
    causal = r >= c
    strict = r > c
    eye = jnp.where(r == c, 1.0, 0.0)

    for n in range(GDN_NC):
        rows = slice(n * CHUNK, (n + 1) * CHUNK)
        k_n = k[rows]
        gc_n = gc[rows]
        grow = gt_scr[n // 2, pl.ds(GDN_HEADS + h, 1), :]
        grow = grow[:, (n % 2) * CHUNK:(n % 2 + 1) * CHUNK]
        g_last = gc_n[CHUNK - 1:CHUNK, :]
        diff = jnp.where(causal, gc_n - grow, 0.0)
        decay = jnp.where(causal, jnp.exp(diff), 0.0)
        ab = _dot_nt(jnp.concatenate([kb[rows], q[rows]], axis=0), k_n)
        low = jnp.where(strict, ab[:CHUNK] * decay, 0.0)
        qk = ab[CHUNK:] * decay
        p = eye - low
        x = _dot(low, low)
        for step in range(5):
            p = p + _dot(p, x)
            if step < 4:
                x = _dot(x, x)
        uw = _dot(p, rhs_all[rows])
        u_scr[h, rows, :] = uw[:, :GDN_DV]
        wq_scr[h, n, 0:CHUNK, :] = uw[:, GDN_DV:].astype(BF16)
        wq_scr[h, n, CHUNK:2 * CHUNK, :] = qd[rows].astype(BF16)
        qk_scr[h, rows, :] = qk.astype(BF16)
        kd_scr[h, rows, :] = (k_n * jnp.exp(g_last - gc_n)).astype(BF16)
        gl_scr[h, n] = jnp.broadcast_to(jnp.exp(g_last), (8, LANES))

    @pl.when(h == GDN_HEADS - 1)
    def _():
        nw = nw_ref[...]

        def chunk_step(n, carry):
            r0 = pl.multiple_of(n * CHUNK, CHUNK)
            for hh in range(GDN_HEADS):
                state = s_scr[hh]
                ws = jnp.dot(wq_scr[hh, n], state.astype(BF16), preferred_element_type=F32)
                v_new = u_scr[hh, pl.ds(r0, CHUNK), :] - ws[:CHUNK]
                vb = v_new.astype(BF16)
                o = ws[CHUNK:] + jnp.dot(qk_scr[hh, pl.ds(r0, CHUNK), :], vb, preferred_element_type=F32)
                upd = lax.dot_general(kd_scr[hh, pl.ds(r0, CHUNK), :], vb, _TN, preferred_element_type=F32)
                s_scr[hh] = state * gl_scr[hh, n][0:1, :] + upd
                on = o * lax.rsqrt(jnp.mean(o * o, axis=-1, keepdims=True) + NORM_EPS) * nw
                zz = z_ref[pl.ds(r0, CHUNK), hh * GDN_DV:(hh + 1) * GDN_DV]
                o_ref[pl.ds(r0, CHUNK), hh * GDN_DV:(hh + 1) * GDN_DV] = (on * _silu(zz)).astype(o_ref.dtype)
            return carry

        lax.fori_loop(0, GDN_NC, chunk_step, 0)


def _gated_deltanet(proj, gates, conv_w, a_log, dt_bias, norm_w):
    t = proj.shape[0]
    tb = GDN_TB
    nh = GDN_HEADS
    pad = lambda a: jnp.zeros((1, LANES), F32).at[0, nh:2 * nh].set(a.astype(F32))
    cur = lambda col: pl.BlockSpec((tb, LANES), lambda i, h: (i, col + h))
    prev = lambda col: pl.BlockSpec((HALO, LANES), lambda i, h: (jnp.maximum(i * (tb // HALO) - 1, 0), col + h))
    cw = lambda col: pl.BlockSpec((CONV_K, LANES), lambda i, h: (0, col + h))
    row = pl.BlockSpec((1, LANES), lambda i, h: (0, 0))
    return pl.pallas_call(
        _gdn_kernel,
        grid=(t // tb, nh),
        in_specs=[cur(0), prev(0), cur(nh), prev(nh), cur(2 * nh), prev(2 * nh),
                  cw(0), cw(nh), cw(2 * nh),
                  pl.BlockSpec((tb, LANES), lambda i, h: (i, 0)), row, row,
                  pl.BlockSpec((tb, GDN_V), lambda i, h: (i, 3)), row],
        out_specs=pl.BlockSpec((tb, GDN_V), lambda i, h: (i, 0)),
        out_shape=jax.ShapeDtypeStruct((t, GDN_V), BF16),
        scratch_shapes=[pltpu.VMEM((tb + HALO, LANES), F32)] * 3 + [
            pltpu.VMEM((tb, LANES), F32), pltpu.VMEM((tb, LANES), F32),
            pltpu.VMEM((tb // LANES, LANES, LANES), F32),
            pltpu.VMEM((nh, GDN_NC, 2 * CHUNK, GDN_DK), BF16),
            pltpu.VMEM((nh, tb, GDN_DV), F32),
            pltpu.VMEM((nh, tb, CHUNK), BF16),
            pltpu.VMEM((nh, tb, GDN_DK), BF16),
            pltpu.VMEM((nh, GDN_NC, 8, LANES), F32),
            pltpu.VMEM((nh, GDN_DK, GDN_DV), F32)],
        compiler_params=_params(("arbitrary", "arbitrary")),
        name="gated_deltanet",
    )(proj, proj, proj, proj, proj, proj, conv_w, conv_w, conv_w,
      gates, pad(a_log), pad(dt_bias), proj, norm_w.reshape(1, GDN_DV))


SSD_TB = 256
SSD_NC = SSD_TB // CHUNK


def _ssd_expand_matrix():
    j = jnp.arange(LANES)[:, None]
    col = jnp.arange(2 * SSD_GW)[None, :]
    head = (col % SSD_GW) // SSD_HEADDIM
    piece = j // SSD_HPG
    is_ac = (col < SSD_GW) & (piece < 3)
    is_dt = (col >= SSD_GW) & (piece >= 3) & (piece < 5)
    return jnp.where((is_ac | is_dt) & (j % SSD_HPG == head), 1.0, 0.0).astype(BF16)


def _ssd_kernel(xc_ref, xp_ref, bc_ref, bp_ref, cc_ref, cp_ref, wx_ref, wb_ref, wc_ref,
                bx_ref, bb_ref, bcn_ref, dt_ref, dtb_ref, alog_ref, dsk_ref, z_ref, nw_ref, ex_ref, o_ref,
                cx_scr, cb_scr, cc_scr, dt_scr, ac_scr, x_scr, e_scr, st_scr):
    i = pl.program_id(0)
    g = pl.program_id(1)
    tb = SSD_TB
    hpg = SSD_HPG
    first = i == 0

    @pl.when(first)
    def _():
        st_scr[g] = jnp.zeros(st_scr.shape[1:], F32)

    @pl.when(g == 0)
    def _():
        dt = _softplus(dt_ref[...] + dtb_ref[...])
        dt_scr[...] = dt
        ac_scr[...] = _dot_exact_lhs(_seg_tri(tb), dt * (-jnp.exp(alog_ref[...])))

    shift = (LANES - g * hpg) % LANES
    lane = lax.broadcasted_iota(jnp.int32, (1, LANES), 1)
    mine = lane < hpg
    dt_g = jnp.where(mine, pltpu.roll(dt_scr[...], shift, 1), 0.0)
    ac_g = jnp.where(mine, pltpu.roll(ac_scr[...], shift, 1), 0.0)

    a1, a2, a3 = _split3(ac_g)
    d1 = dt_g.astype(BF16)
    d2 = (dt_g - d1.astype(F32)).astype(BF16)
    packed = a1.astype(F32)
    for k, piece in enumerate((a2, a3, d1, d2), start=1):
        packed = packed + pltpu.roll(piece.astype(F32), k * hpg, 1)
    e_scr[...] = jnp.dot(packed.astype(BF16), ex_ref[...], preferred_element_type=F32)

    ac_t = [ac_g[s * LANES:(s + 1) * LANES, :].T[0:8, :] for s in range(tb // LANES)]
    ac_tr = [pltpu.roll(a, CHUNK, 1) for a in ac_t]

    x_scr[...] = _silu(_causal_conv(xc_ref, xp_ref, wx_ref, cx_scr, first) + bx_ref[...])
    bm = _silu(_causal_conv(bc_ref, bp_ref, wb_ref, cb_scr, first) + bb_ref[...])
    cm = _silu(_causal_conv(cc_ref, cp_ref, wc_ref, cc_scr, first) + bcn_ref[...])

    r = lax.broadcasted_iota(jnp.int32, (CHUNK, LANES), 0)
    c = lax.broadcasted_iota(jnp.int32, (CHUNK, LANES), 1)
    causal2 = r >= (c % CHUNK)
    left = c < CHUNK
    dsk = dsk_ref[...]
    nw = nw_ref[...]
    gw = SSD_GW

    for n in range(SSD_NC):
        rows = slice(n * CHUNK, (n + 1) * CHUNK)
        x_n, b_n, c_n = x_scr[rows, :], bm[rows], cm[rows]
        acx = e_scr[rows, 0:gw]
        xdt = x_n * e_scr[rows, gw:2 * gw]
        a_last = acx[CHUNK - 1:CHUNK, :]
        state = st_scr[g]
        cb = _dot_nt(c_n, b_n)
        cb2 = jnp.concatenate([cb, cb], axis=1)
        y_off = _dot(c_n, state)
        t_lo, t_hi = (ac_t[n // 2], ac_tr[n // 2]) if n % 2 == 0 else (ac_tr[n // 2], ac_t[n // 2])
        ys = []
        for p in range(hpg // 2):
            lanes = slice(p * LANES, (p + 1) * LANES)
            a_row = jnp.where(lane < CHUNK, t_lo[2 * p:2 * p + 1, :], t_hi[2 * p + 1:2 * p + 2, :])
            decay = jnp.where(causal2, jnp.exp(jnp.where(causal2, acx[:, lanes] - a_row, 0.0)), 0.0)
            xp = xdt[:, lanes]
            x_bd = jnp.concatenate([jnp.where(left, xp, 0.0), jnp.where(left, 0.0, xp)], axis=0)
            ys.append(_dot(cb2 * decay, x_bd))
        y = jnp.concatenate(ys, axis=1) + y_off * jnp.exp(acx) + x_n * dsk
        st_scr[g] = state * jnp.exp(a_last) + _dot_tn(b_n, xdt * jnp.exp(a_last - acx))
        y = y * _silu(z_ref[rows, :])
        y = y * lax.rsqrt(jnp.mean(y * y, axis=-1, keepdims=True) + NORM_EPS) * nw
        o_ref[rows, :] = y.astype(o_ref.dtype)


def _mamba2_ssd(proj, dt_raw, conv_w, conv_b, dt_bias, a_log, d_skip, norm_w):
    t = proj.shape[0]
    tb = SSD_TB
    gw = SSD_GW
    ng = SSD_GROUPS
    x0 = SSD_DINNER // gw
    b0 = 2 * SSD_DINNER // LANES
    c0 = b0 + ng
    pad = lambda a: jnp.zeros((1, LANES), F32).at[0, :SSD_HEADS].set(a.astype(F32))
    prev_idx = lambda i: jnp.maximum(i * (tb // HALO) - 1, 0)
    cbias = conv_b.reshape(1, -1)
    dsk = jnp.repeat(d_skip.astype(F32), SSD_HEADDIM).reshape(1, SSD_DINNER)
    return pl.pallas_call(
        _ssd_kernel,
        grid=(t // tb, ng),
        in_specs=[pl.BlockSpec((tb, gw), lambda i, g: (i, x0 + g)),
                  pl.BlockSpec((HALO, gw), lambda i, g: (prev_idx(i), x0 + g)),
                  pl.BlockSpec((tb, LANES), lambda i, g: (i, b0 + g)),
                  pl.BlockSpec((HALO, LANES), lambda i, g: (prev_idx(i), b0 + g)),
                  pl.BlockSpec((tb, LANES), lambda i, g: (i, c0 + g)),
                  pl.BlockSpec((HALO, LANES), lambda i, g: (prev_idx(i), c0 + g)),
                  pl.BlockSpec((CONV_K, gw), lambda i, g: (0, g)),
                  pl.BlockSpec((CONV_K, LANES), lambda i, g: (0, SSD_DINNER // LANES + g)),
                  pl.BlockSpec((CONV_K, LANES), lambda i, g: (0, SSD_DINNER // LANES + ng + g)),
                  pl.BlockSpec((1, gw), lambda i, g: (0, g)),
                  pl.BlockSpec((1, LANES), lambda i, g: (0, SSD_DINNER // LANES + g)),
                  pl.BlockSpec((1, LANES), lambda i, g: (0, SSD_DINNER // LANES + ng + g)),
                  pl.BlockSpec((tb, LANES), lambda i, g: (i, 0)),
                  pl.BlockSpec((1, LANES), lambda i, g: (0, 0)),
                  pl.BlockSpec((1, LANES), lambda i, g: (0, 0)),
                  pl.BlockSpec((1, gw), lambda i, g: (0, g)),
                  pl.BlockSpec((tb, gw), lambda i, g: (i, g)),
                  pl.BlockSpec((1, gw), lambda i, g: (0, g)),
                  pl.BlockSpec((LANES, 2 * gw), lambda i, g: (0, 0))],
        out_specs=pl.BlockSpec((tb, gw), lambda i, g: (i, g)),
        out_shape=jax.ShapeDtypeStruct((t, SSD_DINNER), BF16),
        scratch_shapes=[pltpu.VMEM((tb + HALO, gw), F32), pltpu.VMEM((tb + HALO, LANES), F32),
                        pltpu.VMEM((tb + HALO, LANES), F32),
                        pltpu.VMEM((tb, LANES), F32), pltpu.VMEM((tb, LANES), F32),
                        pltpu.VMEM((tb, gw), F32), pltpu.VMEM((tb, 2 * gw), F32),
                        pltpu.VMEM((ng, SSD_DSTATE, gw), F32)],
        compiler_params=_params(("arbitrary", "arbitrary")),
        name="mamba2_ssd",
    )(proj, proj, proj, proj, proj, proj, conv_w, conv_w, conv_w, cbias, cbias, cbias,
      dt_raw, pad(dt_bias), pad(a_log), dsk, proj, norm_w.reshape(1, SSD_DINNER), _ssd_expand_matrix())


def _layer_mods(mod, layer):
    d = D_MODEL
    return [mod[layer, k * d:(k + 1) * d].reshape(1, d) for k in range(6)]


def kernel(x, c, mod_w, mod_b, norm_mix_w, norm_mlp_w, mlp_w1, mlp_w2, ab_w_in, gdn_conv_w, gdn_a_log,
           gdn_dt_bias, gdn_norm_w, attn_q_norm_w, attn_k_norm_w, attn_rel_bias, ab_w_out, ssd_w_in,
           ssd_conv_w, ssd_conv_b, ssd_dt_bias, ssd_a_log, ssd_d, ssd_norm_w, ssd_w_out):
    b, t, d = x.shape
    assert b == 1 and d == D_MODEL
    xs = x.reshape(t, d)
    mod = _modulation(c, mod_w, mod_b)

    sh1, sc1, g1, sh2, sc2, g2 = _layer_mods(mod, 0)
    w_in = ab_w_in[0]
    gate_lo = 2 * GDN_QK + 2 * GDN_V
    gate_hi = gate_lo + 2 * GDN_HEADS
    w_main = jnp.concatenate([w_in[:, :gate_lo], w_in[:, gate_hi:]], axis=1)
    w_gate = jnp.pad(w_in[:, gate_lo:gate_hi], ((0, 0), (0, LANES - 2 * GDN_HEADS)))
    proj, gates = _norm_proj(xs, norm_mix_w[0].reshape(1, d), sc1, sh1, w_main, w_gate, tm=512)
    o_a = _gated_deltanet(proj, gates, gdn_conv_w[0], gdn_a_log[0], gdn_dt_bias[0], gdn_norm_w[0])
    a0 = gate_lo // LANES
    o_b = _band_attention(proj, a0, a0 + ATT_HEADS, a0 + 2 * ATT_HEADS,
                          attn_q_norm_w[0], attn_k_norm_w[0], attn_rel_bias[0])
    xs = _out_proj([o_a, o_b], [ab_w_out[0][:GDN_V], ab_w_out[0][GDN_V:]], xs, g1)
    xs = _mlp(xs, norm_mlp_w[0].reshape(1, d), sc2, sh2, g2, mlp_w1[0], mlp_w2[0])

    sh1, sc1, g1, sh2, sc2, g2 = _layer_mods(mod, 1)
    w_in = ssd_w_in[0]
    n_main = 2 * SSD_DINNER + 2 * SSD_GROUPS * SSD_DSTATE
    w_dt = jnp.pad(w_in[:, n_main:], ((0, 0), (0, LANES - SSD_HEADS)))
    proj, dt_raw = _norm_proj(xs, norm_mix_w[1].reshape(1, d), sc1, sh1, w_in[:, :n_main], w_dt, tm=256)
    y = _mamba2_ssd(proj, dt_raw, ssd_conv_w[0], ssd_conv_b[0], ssd_dt_bias[0], ssd_a_log[0],
                    ssd_d[0], ssd_norm_w[0])
    xs = _out_proj([y], [ssd_w_out[0]], xs, g1)
    xs = _mlp(xs, norm_mlp_w[1].reshape(1, d), sc2, sh2, g2, mlp_w1[1], mlp_w2[1])
    return xs.reshape(b, t, d)
```

```python
import functools

import jax
import jax.numpy as jnp
from jax import lax
from jax.experimental import pallas as pl
from jax.experimental.pallas import tpu as pltpu

F32 = jnp.float32
BF16 = jnp.bfloat16

D_MODEL = 2048
CHUNK = 64
NORM_EPS = 1e-6
CONV_K = 4
HALO = 8
GDN_HEADS = 8
GDN_DK = 128
GDN_DV = 128
GDN_QK = GDN_HEADS * GDN_DK
GDN_V = GDN_HEADS * GDN_DV
ATT_HEADS = 8
ATT_DH = 128
ATT_W = ATT_HEADS * ATT_DH
BAND_CHUNKS = 9
REL_CLIP = 256
SSD_DINNER = 2 * D_MODEL
SSD_HEADDIM = 64
SSD_HEADS = SSD_DINNER // SSD_HEADDIM
SSD_GROUPS = 8
SSD_HPG = SSD_HEADS // SSD_GROUPS
SSD_DSTATE = 128
SSD_GW = SSD_HPG * SSD_HEADDIM
D_FF = 4 * D_MODEL
LANES = 128
NEG_BIG = -1e30

VMEM_LIMIT = 56 * 1024 * 1024

_NT = (((1,), (1,)), ((), ()))
_TN = (((0,), (0,)), ((), ()))


def _dot(a, b):
    return jnp.dot(a.astype(BF16), b.astype(BF16), preferred_element_type=F32)


def _dot_nt(a, b):
    return lax.dot_general(a.astype(BF16), b.astype(BF16), _NT, preferred_element_type=F32)


def _dot_tn(a, b):
    return lax.dot_general(a.astype(BF16), b.astype(BF16), _TN, preferred_element_type=F32)


def _split3(a):
    a1 = a.astype(BF16)
    r1 = a - a1.astype(F32)
    a2 = r1.astype(BF16)
    a3 = (r1 - a2.astype(F32)).astype(BF16)
    return a1, a2, a3


def _dot_exact_rhs(a, b_bf16):
    a1, a2, a3 = _split3(a)
    out = jnp.dot(a1, b_bf16, preferred_element_type=F32)
    out += jnp.dot(a2, b_bf16, preferred_element_type=F32)
    out += jnp.dot(a3, b_bf16, preferred_element_type=F32)
    return out


def _dot_exact_lhs(a_bf16, b):
    b1, b2, b3 = _split3(b)
    out = jnp.dot(a_bf16, b1, preferred_element_type=F32)
    out += jnp.dot(a_bf16, b2, preferred_element_type=F32)
    out += jnp.dot(a_bf16, b3, preferred_element_type=F32)
    return out


def _silu(x):
    return x * jax.nn.sigmoid(x)


def _softplus(x):
    return jnp.maximum(x, 0.0) + jnp.log(1.0 + jnp.exp(-jnp.abs(x)))


def _norm_mod(x, nw, sc, sh):
    ms = jnp.mean(x * x, axis=-1, keepdims=True)
    return (x * lax.rsqrt(ms + NORM_EPS) * nw) * (1.0 + sc) + sh


def _seg_tri(n):
    r = lax.broadcasted_iota(jnp.int32, (n, n), 0)
    c = lax.broadcasted_iota(jnp.int32, (n, n), 1)
    return jnp.where((r >= c) & ((r // CHUNK) == (c // CHUNK)), 1.0, 0.0).astype(BF16)


def _params(sem):
    return pltpu.CompilerParams(dimension_semantics=sem, vmem_limit_bytes=VMEM_LIMIT)


def _mod_kernel(c_ref, w_ref, b_ref, o_ref):
    c = c_ref[...]
    o_ref[...] = jnp.dot(_silu(c), w_ref[...], preferred_element_type=F32,
                         precision=lax.Precision.HIGHEST) + b_ref[...]


def _modulation(c, mod_w, mod_b, tn=1024):
    depth, d, n = mod_w.shape
    c8 = jnp.broadcast_to(c.reshape(1, d), (8, d))
    out = pl.pallas_call(
        _mod_kernel,
        grid=(depth, n // tn),
        in_specs=[pl.BlockSpec((8, d), lambda l, j: (0, 0)),
                  pl.BlockSpec((None, d, tn), lambda l, j: (l, 0, j)),
                  pl.BlockSpec((None, 1, tn), lambda l, j: (l, 0, j))],
        out_specs=pl.BlockSpec((None, 8, tn), lambda l, j: (l, 0, j)),
        out_shape=jax.ShapeDtypeStruct((depth, 8, n), F32),
        compiler_params=_params(("arbitrary", "arbitrary")),
        name="modulation",
    )(c8, mod_w, mod_b.reshape(depth, 1, n))
    return out[:, 0, :]


def _proj_kernel(x_ref, nw_ref, sc_ref, sh_ref, w_ref, wg_ref, o_ref, g_ref):
    h = _norm_mod(x_ref[...], nw_ref[...], sc_ref[...], sh_ref[...])
    hb = h.astype(BF16)
    o_ref[...] = jnp.dot(hb, w_ref[...], preferred_element_type=F32)
    ng = g_ref.shape[-1]

    @pl.when(pl.program_id(0) == 0)
    def _():
        hl = (h - hb.astype(F32)).astype(BF16)
        r = jnp.dot(hb, wg_ref[...], preferred_element_type=F32)
        g_ref[...] = r[:, :ng] + r[:, ng:] + jnp.dot(hl, wg_ref[:, :ng], preferred_element_type=F32)

    @pl.when(pl.program_id(0) != 0)
    def _():
        g_ref[...] = jnp.zeros_like(g_ref)


def _norm_proj(x, nw, sc, sh, w, wg, tm, n_split=2):
    t, d = x.shape
    n = w.shape[1]
    ng = wg.shape[1]
    tn = n // n_split
    wg1 = wg.astype(BF16)
    wg2 = (wg - wg1.astype(F32)).astype(BF16)
    row = lambda s, i: (0, 0)
    out, gates = pl.pallas_call(
        _proj_kernel,
        grid=(n_split, t // tm),
        in_specs=[pl.BlockSpec((tm, d), lambda s, i: (i, 0)),
                  pl.BlockSpec((1, d), row), pl.BlockSpec((1, d), row), pl.BlockSpec((1, d), row),
                  pl.BlockSpec((d, tn), lambda s, i: (0, s), pipeline_mode=pl.Buffered(1)),
                  pl.BlockSpec((d, 2 * ng), row, pipeline_mode=pl.Buffered(1))],
        out_specs=[pl.BlockSpec((tm, tn), lambda s, i: (i, s)),
                   pl.BlockSpec((None, tm, ng), lambda s, i: (s, i, 0))],
        out_shape=[jax.ShapeDtypeStruct((t, n), F32), jax.ShapeDtypeStruct((n_split, t, ng), F32)],
        compiler_params=_params(("arbitrary", "arbitrary")),
        name="norm_proj",
    )(x, nw, sc, sh, w.astype(BF16), jnp.concatenate([wg1, wg2], axis=1))
    return out, gates[0]


def _out_kernel(n_in, *refs):
    y_refs = refs[:n_in]
    w_refs = refs[n_in:2 * n_in]
    x_ref, g_ref, o_ref = refs[2 * n_in:]
    acc = jnp.dot(y_refs[0][...], w_refs[0][...], preferred_element_type=F32)
    for y_ref, w_ref in zip(y_refs[1:], w_refs[1:]):
        acc += jnp.dot(y_ref[...], w_ref[...], preferred_element_type=F32)
    o_ref[...] = x_ref[...] + g_ref[...] * acc


def _out_proj(ys, ws, x, gate, tm=512, tn=1024):
    t, d = x.shape
    n_in = len(ys)
    in_specs = [pl.BlockSpec((tm, y.shape[1]), lambda i, j: (i, 0)) for y in ys]
    in_specs += [pl.BlockSpec((w.shape[0], tn), lambda i, j: (0, j)) for w in ws]
    in_specs += [pl.BlockSpec((tm, tn), lambda i, j: (i, j)), pl.BlockSpec((1, tn), lambda i, j: (0, j))]
    return pl.pallas_call(
        functools.partial(_out_kernel, n_in),
        grid=(t // tm, d // tn),
        in_specs=in_specs,
        out_specs=pl.BlockSpec((tm, tn), lambda i, j: (i, j)),
        out_shape=jax.ShapeDtypeStruct((t, d), F32),
        compiler_params=_params(("parallel", "arbitrary")),
        name="out_proj",
    )(*ys, *[w.astype(BF16) for w in ws], x, gate)


def _mlp_kernel(x_ref, nw_ref, sc_ref, sh_ref, g_ref, w1_ref, w2_ref, o_ref, h_scr):
    f = pl.program_id(1)

    @pl.when(f == 0)
    def _():
        h_scr[...] = _norm_mod(x_ref[...], nw_ref[...], sc_ref[...], sh_ref[...]).astype(BF16)
        o_ref[...] = jnp.zeros_like(o_ref)

    a = jnp.maximum(jnp.dot(h_scr[...], w1_ref[...], preferred_element_type=F32), 0.0)
    o_ref[...] += jnp.dot((a * a).astype(BF16), w2_ref[...], preferred_element_type=F32)

    @pl.when(f == pl.num_programs(1) - 1)
    def _():
        o_ref[...] = x_ref[...] + g_ref[...] * o_ref[...]


def _mlp(x, nw, sc, sh, gate, w1, w2, tm=1024, tf=512):
    t, d = x.shape
    ff = w1.shape[1]
    row = lambda i, f: (0, 0)
    return pl.pallas_call(
        _mlp_kernel,
        grid=(t // tm, ff // tf),
        in_specs=[pl.BlockSpec((tm, d), lambda i, f: (i, 0), pipeline_mode=pl.Buffered(1)),
                  pl.BlockSpec((1, d), row), pl.BlockSpec((1, d), row), pl.BlockSpec((1, d), row),
                  pl.BlockSpec((1, d), row),
                  pl.BlockSpec((d, tf), lambda i, f: (0, f)),
                  pl.BlockSpec((tf, d), lambda i, f: (f, 0))],
        out_specs=pl.BlockSpec((tm, d), lambda i, f: (i, 0)),
        out_shape=jax.ShapeDtypeStruct((t, d), F32),
        scratch_shapes=[pltpu.VMEM((tm, d), BF16)],
        compiler_params=_params(("parallel", "arbitrary")),
        name="mlp",
    )(x, nw, sc, sh, gate, w1.astype(BF16), w2.astype(BF16))


ATT_TQ = 256
ATT_KB = 3
ATT_HG = 4


def _attn_kernel(q_ref, k_ref, v_ref, bias_ref, qw_ref, kw_ref, o_ref, kn_scr, v_scr):
    i = pl.program_id(0)

    @pl.when(i == 0)
    def _():
        kn_scr[...] = jnp.zeros_like(kn_scr)
        v_scr[...] = jnp.zeros_like(v_scr)

    def rms(x, w):
        return x * lax.rsqrt(jnp.mean(x * x, axis=-1, keepdims=True) + NORM_EPS) * w

    slots = [lax.rem(i + 1 + b, ATT_KB) for b in range(ATT_KB)]
    kw = kw_ref[...]
    qw = qw_ref[...]
    for h in range(ATT_HEADS):
        cols = slice(h * ATT_DH, (h + 1) * ATT_DH)
        kn_scr[slots[-1], :, cols] = rms(k_ref[:, cols], kw).astype(BF16)
    v_scr[slots[-1]] = v_ref[...].astype(BF16)

    for h0 in range(0, ATT_HEADS, ATT_HG):
        hs = range(h0, h0 + ATT_HG)
        col = {h: slice(h * ATT_DH, (h + 1) * ATT_DH) for h in hs}
        q = {h: (rms(q_ref[:, col[h]], qw) * (ATT_DH ** -0.5)).astype(BF16) for h in hs}
        s = {}
        for h in hs:
            for b in range(ATT_KB):
                sb = lax.dot_general(q[h], kn_scr[slots[b], :, col[h]], _NT, preferred_element_type=F32)
                sb = sb + bias_ref[h, :, b * ATT_TQ:(b + 1) * ATT_TQ]
                s[h, b] = jnp.where(i >= ATT_KB - 1 - b, sb, NEG_BIG)
        p, l = {}, {}
        for h in hs:
            m = jnp.max(jnp.maximum(jnp.maximum(s[h, 0], s[h, 1]), s[h, 2]), axis=-1, keepdims=True)
            for b in range(ATT_KB):
                p[h, b] = jnp.exp(s[h, b] - m)
            l[h] = jnp.sum(p[h, 0] + p[h, 1] + p[h, 2], axis=-1, keepdims=True)
        for h in hs:
            o = jnp.dot(p[h, 0].astype(BF16), v_scr[slots[0], :, col[h]], preferred_element_type=F32)
            for b in range(1, ATT_KB):
                o += jnp.dot(p[h, b].astype(BF16), v_scr[slots[b], :, col[h]], preferred_element_type=F32)
            o_ref[:, col[h]] = (o / l[h]).astype(o_ref.dtype)


def _attn_bias_table(rel_bias):
    nk = ATT_KB * ATT_TQ
    span = ATT_TQ + nk
    dist = jnp.arange(span) - (ATT_TQ - 1) - (ATT_KB - 1) * ATT_TQ
    ext = rel_bias.astype(F32)[:, jnp.clip(dist, -REL_CLIP, REL_CLIP) + REL_CLIP]
    flat = jnp.tile(ext, (1, ATT_TQ))[:, :ATT_TQ * (span - 1)]
    tab = flat.reshape(-1, ATT_TQ, span - 1)[:, :, ATT_TQ - 1:ATT_TQ - 1 + nk]
    r = jnp.arange(ATT_TQ)[:, None]
    m = jnp.arange(nk)[None, :]
    qc = r // CHUNK
    kc = m // CHUNK
    in_band = (kc >= qc) & (kc <= qc + BAND_CHUNKS - 1)
    return jnp.where(in_band[None], tab, NEG_BIG)


def _band_attention(proj, q_blk, k_blk, v_blk, q_norm_w, k_norm_w, rel_bias):
    t = proj.shape[0]
    bias = _attn_bias_table(rel_bias)
    blk = (ATT_TQ, ATT_W)
    return pl.pallas_call(
        _attn_kernel,
        grid=(t // ATT_TQ,),
        in_specs=[pl.BlockSpec(blk, lambda i: (i, q_blk)),
                  pl.BlockSpec(blk, lambda i: (i, k_blk)),
                  pl.BlockSpec(blk, lambda i: (i, v_blk)),
                  pl.BlockSpec(bias.shape, lambda i: (0, 0, 0), pipeline_mode=pl.Buffered(1)),
                  pl.BlockSpec((1, ATT_DH), lambda i: (0, 0)),
                  pl.BlockSpec((1, ATT_DH), lambda i: (0, 0))],
        out_specs=pl.BlockSpec(blk, lambda i: (i, 0)),
        out_shape=jax.ShapeDtypeStruct((t, ATT_W), BF16),
        scratch_shapes=[pltpu.VMEM((ATT_KB, ATT_TQ, ATT_W), BF16), pltpu.VMEM((ATT_KB, ATT_TQ, ATT_W), BF16)],
        compiler_params=_params(("arbitrary",)),
        name="band_attention",
    )(proj, proj, proj, bias, q_norm_w.reshape(1, ATT_DH), k_norm_w.reshape(1, ATT_DH))


GDN_TB = 1024
GDN_NC = GDN_TB // CHUNK
GDN_HPS = 4
GDN_GRP = 4


def _causal_conv(cur_ref, prev_ref, w_ref, scr, first_block):
    rows = cur_ref.shape[0]
    prev = prev_ref[...]
    scr[0:HALO, :] = jnp.where(first_block, jnp.zeros_like(prev), prev)
    scr[HALO:HALO + rows, :] = cur_ref[...]
    w = w_ref[...]
    acc = scr[pl.ds(HALO - (CONV_K - 1), rows), :] * w[0:1, :]
    for j in range(1, CONV_K):
        acc += scr[pl.ds(HALO - (CONV_K - 1) + j, rows), :] * w[j:j + 1, :]
    return acc


def _block_diag(h):
    blk = lax.broadcasted_iota(jnp.int32, (1, h.shape[1]), 1) // CHUNK
    return jnp.concatenate([jnp.where(blk == j, h, jnp.zeros_like(h)) for j in range(h.shape[1] // CHUNK)], axis=0)


def _gdn_kernel(qc_ref, qp_ref, kc_ref, kp_ref, vc_ref, vp_ref, wq_ref, wk_ref, wv_ref,
                gate_ref, alog_ref, dtb_ref, z_ref, nw_ref, o_ref,
                cq_scr, ck_scr, cv_scr, beta_scr, gcum_scr, gt_scr,
                wq_scr, u_scr, qk_scr, a_scr, b_scr, s_scr):
    i = pl.program_id(0)
    hp = pl.program_id(1)
    tb = GDN_TB
    first = i == 0

    @pl.when(hp == 0)
    def _():
        gate = gate_ref[...]
        beta_scr[...] = jax.nn.sigmoid(gate)
        g = -jnp.exp(alog_ref[...]) * _softplus(gate + dtb_ref[...])
        gcum = _dot_exact_lhs(_seg_tri(tb), g)
        gcum_scr[...] = gcum
        for s in range(tb // LANES):
            gt_scr[s] = gcum[s * LANES:(s + 1) * LANES, :].T

    lane = lax.broadcasted_iota(jnp.int32, (1, LANES), 1)
    r = lax.broadcasted_iota(jnp.int32, (CHUNK, CHUNK), 0)
    c = lax.broadcasted_iota(jnp.int32, (CHUNK, CHUNK), 1)
    causal = r >= c
    strict = r > c
    rh = lax.broadcasted_iota(jnp.int32, (CHUNK, GDN_GRP * CHUNK), 0)
    ch = lax.broadcasted_iota(jnp.int32, (CHUNK, GDN_GRP * CHUNK), 1)
    eye_h = jnp.where(rh == ch % CHUNK, 1.0, 0.0)
    nw = nw_ref[...]
    heads = range(GDN_HPS)
    groups = range(GDN_NC // GDN_GRP)
    gw = GDN_GRP * CHUNK

    def l2n(x):
        return x * lax.rsqrt(jnp.sum(x * x, axis=-1, keepdims=True) + NORM_EPS)

    qa = _silu(_causal_conv(qc_ref, qp_ref, wq_ref, cq_scr, first))
    ka = _silu(_causal_conv(kc_ref, kp_ref, wk_ref, ck_scr, first))
    va = _silu(_causal_conv(vc_ref, vp_ref, wv_ref, cv_scr, first))

    gcs, ks, rhs, low_hs, p_hs, x_hs = {}, {}, {}, {}, {}, {}
    for hl in heads:
        h = hp * GDN_HPS + hl
        cols = slice(hl * LANES, (hl + 1) * LANES)
        beta = jnp.sum(jnp.where(lane == h, beta_scr[...], 0.0), axis=1, keepdims=True)
        gc = jnp.sum(jnp.where(lane == GDN_HEADS + h, gcum_scr[...], 0.0), axis=1, keepdims=True)
        q = l2n(qa[:, cols]) * (GDN_DK ** -0.5)
        k = l2n(ka[:, cols])
        eg = jnp.exp(gc)
        kb = k * beta
        rhs[hl] = jnp.concatenate([va[:, cols] * beta, kb * eg], axis=1).astype(BF16)
        qd = (q * eg).astype(BF16)
        gcs[hl], ks[hl] = gc, k
        for grp in groups:
            lows = []
            for j in range(GDN_GRP):
                n = grp * GDN_GRP + j
                rows = slice(n * CHUNK, (n + 1) * CHUNK)
                gc_n = gc[rows]
                grow = gt_scr[n // 2, pl.ds(GDN_HEADS + h, 1), :]
                grow = grow[:, (n % 2) * CHUNK:(n % 2 + 1) * CHUNK]
                decay = jnp.where(causal, jnp.exp(jnp.where(causal, gc_n - grow, 0.0)), 0.0)
                ab = _dot_nt(jnp.concatenate([kb[rows], q[rows]], axis=0), k[rows])
                lows.append(jnp.where(strict, ab[:CHUNK] * decay, 0.0))
                qk_scr[hl, rows, :] = (ab[CHUNK:] * decay).astype(BF16)
                wq_scr[hl, n, CHUNK:2 * CHUNK, :] = qd[rows]
            low_hs[hl, grp] = jnp.concatenate(lows, axis=1)

    for key, low_h in low_hs.items():
        xb = low_h.astype(BF16)
        p_hs[key] = eye_h - low_h
        x_hs[key] = jnp.dot(xb, _block_diag(xb), preferred_element_type=F32)
    for step in range(5):
        for key in low_hs:
            xb = x_hs[key].astype(BF16)
            x_bd = _block_diag(xb)
            p_hs[key] = p_hs[key] + jnp.dot(p_hs[key].astype(BF16), x_bd, preferred_element_type=F32)
            if step < 4:
                x_hs[key] = jnp.dot(xb, x_bd, preferred_element_type=F32)

    for (hl, grp), p_h in p_hs.items():
        g0 = grp * gw
        uw = jnp.dot(_block_diag(p_h.astype(BF16)), rhs[hl][g0:g0 + gw], preferred_element_type=F32)
        u_scr[hl, g0:g0 + gw, :] = uw[:, :GDN_DV]
        for j in range(GDN_GRP):
            n = grp * GDN_GRP + j
            rows = slice(n * CHUNK, (n + 1) * CHUNK)
            uw_n = uw[j * CHUNK:(j + 1) * CHUNK]
            wq_scr[hl, n, 0:CHUNK, :] = uw_n[:, GDN_DV:].astype(BF16)
            gc_n = gcs[hl][rows]
            kd = ks[hl][rows] * jnp.exp(gc_n[CHUNK - 1:CHUNK, :] - gc_n)
            ba = _dot_tn(kd, uw_n)
            b_scr[hl, n] = ba[:, :GDN_DV]
            a_scr[hl, n] = ba[:, GDN_DV:].astype(BF16)

    states = [jnp.where(first, 0.0, s_scr[hp * GDN_HPS + hl]) for hl in heads]
    for n in range(GDN_NC):
        rows = slice(n * CHUNK, (n + 1) * CHUNK)
        for hl in heads:
            cols = slice(hl * LANES, (hl + 1) * LANES)
            sb = states[hl].astype(BF16)
            ws = jnp.dot(wq_scr[hl, n], sb, preferred_element_type=F32)
            g_last = jnp.exp(gcs[hl][(n + 1) * CHUNK - 1:(n + 1) * CHUNK, :])
            states[hl] = states[hl] * g_last - jnp.dot(a_scr[hl, n], sb, preferred_element_type=F32) + b_scr[hl, n]
            v_new = u_scr[hl, rows, :] - ws[:CHUNK]
            o = ws[CHUNK:] + jnp.dot(qk_scr[hl, rows, :], v_new.astype(BF16), preferred_element_type=F32)
            on = o * lax.rsqrt(jnp.mean(o * o, axis=-1, keepdims=True) + NORM_EPS) * nw
            o_ref[rows, cols] = (on * _silu(z_ref[rows, cols])).astype(o_ref.dtype)
    for hl in heads:
        s_scr[hp * GDN_HPS + hl] = states[hl]


def _gated_deltanet(proj, gates, conv_w, a_log, dt_bias, norm_w):
    t = proj.shape[0]
    tb = GDN_TB
    nh = GDN_HEADS
    w = GDN_HPS * LANES
    nb = GDN_QK // w
    pad = lambda a: jnp.zeros((1, LANES), F32).at[0, nh:2 * nh].set(a.astype(F32))
    cur = lambda sec: pl.BlockSpec((tb, w), lambda i, h: (i, sec * nb + h))
    prev = lambda sec: pl.BlockSpec((HALO, w), lambda i, h: (jnp.maximum(i * (tb // HALO) - 1, 0), sec * nb + h))
    cw = lambda sec: pl.BlockSpec((CONV_K, w), lambda i, h: (0, sec * nb + h))
    row = pl.BlockSpec((1, LANES), lambda i, h: (0, 0))
    return pl.pallas_call(
        _gdn_kernel,
        grid=(t // tb, nh // GDN_HPS),
        in_specs=[cur(0), prev(0), cur(1), prev(1), cur(2), prev(2), cw(0), cw(1), cw(2),
                  pl.BlockSpec((tb, LANES), lambda i, h: (i, 0)), row, row, cur(3), row],
        out_specs=pl.BlockSpec((tb, w), lambda i, h: (i, h)),
        out_shape=jax.ShapeDtypeStruct((t, GDN_V), BF16),
        scratch_shapes=[pltpu.VMEM((tb + HALO, w), F32)] * 3 + [
            pltpu.VMEM((tb, LANES), F32), pltpu.VMEM((tb, LANES), F32),
            pltpu.VMEM((tb // LANES, LANES, LANES), F32),
            pltpu.VMEM((GDN_HPS, GDN_NC, 2 * CHUNK, GDN_DK), BF16),
            pltpu.VMEM((GDN_HPS, tb, GDN_DV), F32),
            pltpu.VMEM((GDN_HPS, tb, CHUNK), BF16),
            pltpu.VMEM((GDN_HPS, GDN_NC, GDN_DK, GDN_DK), BF16),
            pltpu.VMEM((GDN_HPS, GDN_NC, GDN_DK, GDN_DV), F32),
            pltpu.VMEM((nh, GDN_DK, GDN_DV), F32)],
        compiler_params=_params(("arbitrary", "arbitrary")),
        name="gated_deltanet",
    )(proj, proj, proj, proj, proj, proj, conv_w, conv_w, conv_w,
      gates, pad(a_log), pad(dt_bias), proj, norm_w.reshape(1, GDN_DV))


SSD_TB = 256
SSD_NC = SSD_TB // CHUNK


def _ssd_expand_matrix():
    j = jnp.arange(LANES)[:, None]
    col = jnp.arange(2 * SSD_GW)[None, :]
    head = (col % SSD_GW) // SSD_HEADDIM
    piece = j // SSD_HPG
    is_ac = (col < SSD_GW) & (piece < 3)
    is_dt = (col >= SSD_GW) & (piece >= 3) & (piece < 5)
    return jnp.where((is_ac | is_dt) & (j % SSD_HPG == head), 1.0, 0.0).astype(BF16)


def _ssd_kernel(xc_ref, xp_ref, bc_ref, bp_ref, cc_ref, cp_ref, wx_ref, wb_ref, wc_ref,
                bx_ref, bb_ref, bcn_ref, dt_ref, dtb_ref, alog_ref, dsk_ref, z_ref, nw_ref, ex_ref, o_ref,
                cx_scr, cb_scr, cc_scr, dt_scr, ac_scr, x_scr, e_scr, st_scr):
    i = pl.program_id(0)
    g = pl.program_id(1)
    tb = SSD_TB
    hpg = SSD_HPG
    first = i == 0

    @pl.when(first)
    def _():
        st_scr[g] = jnp.zeros(st_scr.shape[1:], F32)

    @pl.when(g == 0)
    def _():
        dt = _softplus(dt_ref[...] + dtb_ref[...])
        dt_scr[...] = dt
        ac_scr[...] = _dot_exact_lhs(_seg_tri(tb), dt * (-jnp.exp(alog_ref[...])))

    shift = (LANES - g * hpg) % LANES
    lane = lax.broadcasted_iota(jnp.int32, (1, LANES), 1)
    mine = lane < hpg
    dt_g = jnp.where(mine, pltpu.roll(dt_scr[...], shift, 1), 0.0)
    ac_g = jnp.where(mine, pltpu.roll(ac_scr[...], shift, 1), 0.0)

    a1, a2, a3 = _split3(ac_g)
    d1 = dt_g.astype(BF16)
    d2 = (dt_g - d1.astype(F32)).astype(BF16)
    packed = a1.astype(F32)
    for k, piece in enumerate((a2, a3, d1, d2), start=1):
        packed = packed + pltpu.roll(piece.astype(F32), k * hpg, 1)
    e_scr[...] = jnp.dot(packed.astype(BF16), ex_ref[...], preferred_element_type=F32)

    ac_t = [ac_g[s * LANES:(s + 1) * LANES, :].T[0:8, :] for s in range(tb // LANES)]
    ac_tr = [pltpu.roll(a, CHUNK, 1) for a in ac_t]

    x_scr[...] = _silu(_causal_conv(xc_ref, xp_ref, wx_ref, cx_scr, first) + bx_ref[...])
    bm = _silu(_causal_conv(bc_ref, bp_ref, wb_ref, cb_scr, first) + bb_ref[...])
    cm = _silu(_causal_conv(cc_ref, cp_ref, wc_ref, cc_scr, first) + bcn_ref[...])

    r = lax.broadcasted_iota(jnp.int32, (CHUNK, LANES), 0)
    c = lax.broadcasted_iota(jnp.int32, (CHUNK, LANES), 1)
    causal2 = r >= (c % CHUNK)
    left = c < CHUNK
    dsk = dsk_ref[...]
    nw = nw_ref[...]
    gw = SSD_GW

    for n in range(SSD_NC):
        rows = slice(n * CHUNK, (n + 1) * CHUNK)
        x_n, b_n, c_n = x_scr[rows, :], bm[rows], cm[rows]
        acx = e_scr[rows, 0:gw]
        xdt = x_n * e_scr[rows, gw:2 * gw]
        a_last = acx[CHUNK - 1:CHUNK, :]
        state = st_scr[g]
        cb = _dot_nt(c_n, b_n)
        cb2 = jnp.concatenate([cb, cb], axis=1)
        y_off = _dot(c_n, state)
        t_lo, t_hi = (ac_t[n // 2], ac_tr[n // 2]) if n % 2 == 0 else (ac_tr[n // 2], ac_t[n // 2])
        ys = []
        for p in range(hpg // 2):
            lanes = slice(p * LANES, (p + 1) * LANES)
            a_row = jnp.where(lane < CHUNK, t_lo[2 * p:2 * p + 1, :], t_hi[2 * p + 1:2 * p + 2, :])
            decay = jnp.where(causal2, jnp.exp(jnp.where(causal2, acx[:, lanes] - a_row, 0.0)), 0.0)
            xp = xdt[:, lanes]
            x_bd = jnp.concatenate([jnp.where(left, xp, 0.0), jnp.where(left, 0.0, xp)], axis=0)
            ys.append(_dot(cb2 * decay, x_bd))
        y = jnp.concatenate(ys, axis=1) + y_off * jnp.exp(acx) + x_n * dsk
        st_scr[g] = state * jnp.exp(a_last) + _dot_tn(b_n, xdt * jnp.exp(a_last - acx))
        y = y * _silu(z_ref[rows, :])
        y = y * lax.rsqrt(jnp.mean(y * y, axis=-1, keepdims=True) + NORM_EPS) * nw
        o_ref[rows, :] = y.astype(o_ref.dtype)


def _mamba2_ssd(proj, dt_raw, conv_w, conv_b, dt_bias, a_log, d_skip, norm_w):
    t = proj.shape[0]
    tb = SSD_TB
    gw = SSD_GW
    ng = SSD_GROUPS
    x0 = SSD_DINNER // gw
    b0 = 2 * SSD_DINNER // LANES
    c0 = b0 + ng
    pad = lambda a: jnp.zeros((1, LANES), F32).at[0, :SSD_HEADS].set(a.astype(F32))
    prev_idx = lambda i: jnp.maximum(i * (tb // HALO) - 1, 0)
    cbias = conv_b.reshape(1, -1)
    dsk = jnp.repeat(d_skip.astype(F32), SSD_HEADDIM).reshape(1, SSD_DINNER)
    return pl.pallas_call(
        _ssd_kernel,
        grid=(t // tb, ng),
        in_specs=[pl.BlockSpec((tb, gw), lambda i, g: (i, x0 + g)),
                  pl.BlockSpec((HALO, gw), lambda i, g: (prev_idx(i), x0 + g)),
                  pl.BlockSpec((tb, LANES), lambda i, g: (i, b0 + g)),
                  pl.BlockSpec((HALO, LANES), lambda i, g: (prev_idx(i), b0 + g)),
                  pl.BlockSpec((tb, LANES), lambda i, g: (i, c0 + g)),
                  pl.BlockSpec((HALO, LANES), lambda i, g: (prev_idx(i), c0 + g)),
                  pl.BlockSpec((CONV_K, gw), lambda i, g: (0, g)),
                  pl.BlockSpec((CONV_K, LANES), lambda i, g: (0, SSD_DINNER // LANES + g)),
                  pl.BlockSpec((CONV_K, LANES), lambda i, g: (0, SSD_DINNER // LANES + ng + g)),
                  pl.BlockSpec((1, gw), lambda i, g: (0, g)),
                  pl.BlockSpec((1, LANES), lambda i, g: (0, SSD_DINNER // LANES + g)),
                  pl.BlockSpec((1, LANES), lambda i, g: (0, SSD_DINNER // LANES + ng + g)),
                  pl.BlockSpec((tb, LANES), lambda i, g: (i, 0)),
                  pl.BlockSpec((1, LANES), lambda i, g: (0, 0)),
                  pl.BlockSpec((1, LANES), lambda i, g: (0, 0)),
                  pl.BlockSpec((1, gw), lambda i, g: (0, g)),
                  pl.BlockSpec((tb, gw), lambda i, g: (i, g)),
                  pl.BlockSpec((1, gw), lambda i, g: (0, g)),
                  pl.BlockSpec((LANES, 2 * gw), lambda i, g: (0, 0))],
        out_specs=pl.BlockSpec((tb, gw), lambda i, g: (i, g)),
        out_shape=jax.ShapeDtypeStruct((t, SSD_DINNER), BF16),
        scratch_shapes=[pltpu.VMEM((tb + HALO, gw), F32), pltpu.VMEM((tb + HALO, LANES), F32),
                        pltpu.VMEM((tb + HALO, LANES), F32),
                        pltpu.VMEM((tb, LANES), F32), pltpu.VMEM((tb, LANES), F32),
                        pltpu.VMEM((tb, gw), F32), pltpu.VMEM((tb, 2 * gw), F32),
                        pltpu.VMEM((ng, SSD_DSTATE, gw), F32)],
        compiler_params=_params(("arbitrary", "arbitrary")),
        name="mamba2_ssd",
    )(proj, proj, proj, proj, proj, proj, conv_w, conv_w, conv_w, cbias, cbias, cbias,
      dt_raw, pad(dt_bias), pad(a_log), dsk, proj, norm_w.reshape(1, SSD_DINNER), _ssd_expand_matrix())


def _layer_mods(mod, layer):
    d = D_MODEL
    return [mod[layer, k * d:(k + 1) * d].reshape(1, d) for k in range(6)]


def kernel(x, c, mod_w, mod_b, norm_mix_w, norm_mlp_w, mlp_w1, mlp_w2, ab_w_in, gdn_conv_w, gdn_a_log,
           gdn_dt_bias, gdn_norm_w, attn_q_norm_w, attn_k_norm_w, attn_rel_bias, ab_w_out, ssd_w_in,
           ssd_conv_w, ssd_conv_b, ssd_dt_bias, ssd_a_log, ssd_d, ssd_norm_w, ssd_w_out):
    b, t, d = x.shape
    assert b == 1 and d == D_MODEL
    xs = x.reshape(t, d)
    mod = _modulation(c, mod_w, mod_b)

    sh1, sc1, g1, sh2, sc2, g2 = _layer_mods(mod, 0)
    w_in = ab_w_in[0]
    gate_lo = 2 * GDN_QK + 2 * GDN_V
    gate_hi = gate_lo + 2 * GDN_HEADS
    w_main = jnp.concatenate([w_in[:, :gate_lo], w_in[:, gate_hi:]], axis=1)
    w_gate = jnp.pad(w_in[:, gate_lo:gate_hi], ((0, 0), (0, LANES - 2 * GDN_HEADS)))
    proj, gates = _norm_proj(xs, norm_mix_w[0].reshape(1, d), sc1, sh1, w_main, w_gate, tm=512)
    o_a = _gated_deltanet(proj, gates, gdn_conv_w[0], gdn_a_log[0], gdn_dt_bias[0], gdn_norm_w[0])
    a0 = gate_lo // ATT_W
    o_b = _band_attention(proj, a0, a0 + 1, a0 + 2, attn_q_norm_w[0], attn_k_norm_w[0], attn_rel_bias[0])
    xs = _out_proj([o_a, o_b], [ab_w_out[0][:GDN_V], ab_w_out[0][GDN_V:]], xs, g1)
    xs = _mlp(xs, norm_mlp_w[0].reshape(1, d), sc2, sh2, g2, mlp_w1[0], mlp_w2[0])

    sh1, sc1, g1, sh2, sc2, g2 = _layer_mods(mod, 1)
    w_in = ssd_w_in[0]
    n_main = 2 * SSD_DINNER + 2 * SSD_GROUPS * SSD_DSTATE
    w_dt = jnp.pad(w_in[:, n_main:], ((0, 0), (0, LANES - SSD_HEADS)))
    proj, dt_raw = _norm_proj(xs, norm_mix_w[1].reshape(1, d), sc1, sh1, w_in[:, :n_main], w_dt, tm=256)
    y = _mamba2_ssd(proj, dt_raw, ssd_conv_w[0], ssd_conv_b[0], ssd_dt_bias[0], ssd_a_log[0],
                    ssd_d[0], ssd_norm_w[0])
    xs = _out_proj([y], [ssd_w_out[0]], xs, g1)
    xs = _mlp(xs, norm_mlp_w[1].reshape(1, d), sc2, sh2, g2, mlp_w1[1], mlp_w2[1])
    return xs.reshape(b, t, d)
```

```python
import functools

import jax
import jax.numpy as jnp
from jax import lax
from jax.experimental import pallas as pl
from jax.experimental.pallas import tpu as pltpu

F32 = jnp.float32
BF16 = jnp.bfloat16

D_MODEL = 2048
CHUNK = 64
NORM_EPS = 1e-6
CONV_K = 4
HALO = 8
GDN_HEADS = 8
GDN_DK = 128
GDN_DV = 128
GDN_QK = GDN_HEADS * GDN_DK
GDN_V = GDN_HEADS * GDN_DV
ATT_HEADS = 8
ATT_DH = 128
ATT_W = ATT_HEADS * ATT_DH
BAND_CHUNKS = 9
REL_CLIP = 256
SSD_DINNER = 2 * D_MODEL
SSD_HEADDIM = 64
SSD_HEADS = SSD_DINNER // SSD_HEADDIM
SSD_GROUPS = 8
SSD_HPG = SSD_HEADS // SSD_GROUPS
SSD_DSTATE = 128
SSD_GW = SSD_HPG * SSD_HEADDIM
D_FF = 4 * D_MODEL
LANES = 128
NEG_BIG = -1e30

VMEM_LIMIT = 56 * 1024 * 1024

_NT = (((1,), (1,)), ((), ()))
_TN = (((0,), (0,)), ((), ()))


def _dot(a, b):
    return jnp.dot(a.astype(BF16), b.astype(BF16), preferred_element_type=F32)


def _dot_nt(a, b):
    return lax.dot_general(a.astype(BF16), b.astype(BF16), _NT, preferred_element_type=F32)


def _dot_tn(a, b):
    return lax.dot_general(a.astype(BF16), b.astype(BF16), _TN, preferred_element_type=F32)


def _split3(a):
    a1 = a.astype(BF16)
    r1 = a - a1.astype(F32)
    a2 = r1.astype(BF16)
    a3 = (r1 - a2.astype(F32)).astype(BF16)
    return a1, a2, a3


def _dot_exact_rhs(a, b_bf16):
    a1, a2, a3 = _split3(a)
    out = jnp.dot(a1, b_bf16, preferred_element_type=F32)
    out += jnp.dot(a2, b_bf16, preferred_element_type=F32)
    out += jnp.dot(a3, b_bf16, preferred_element_type=F32)
    return out


def _dot_exact_lhs(a_bf16, b):
    b1, b2, b3 = _split3(b)
    out = jnp.dot(a_bf16, b1, preferred_element_type=F32)
    out += jnp.dot(a_bf16, b2, preferred_element_type=F32)
    out += jnp.dot(a_bf16, b3, preferred_element_type=F32)
    return out


def _silu(x):
    return x * jax.nn.sigmoid(x)


def _softplus(x):
    return jnp.maximum(x, 0.0) + jnp.log(1.0 + jnp.exp(-jnp.abs(x)))


def _norm_mod(x, nw, sc, sh):
    ms = jnp.mean(x * x, axis=-1, keepdims=True)
    return (x * lax.rsqrt(ms + NORM_EPS) * nw) * (1.0 + sc) + sh


def _seg_tri(n):
    r = lax.broadcasted_iota(jnp.int32, (n, n), 0)
    c = lax.broadcasted_iota(jnp.int32, (n, n), 1)
    return jnp.where((r >= c) & ((r // CHUNK) == (c // CHUNK)), 1.0, 0.0).astype(BF16)


def _params(sem):
    return pltpu.CompilerParams(dimension_semantics=sem, vmem_limit_bytes=VMEM_LIMIT)


def _mod_kernel(c_ref, w_ref, b_ref, o_ref):
    c = c_ref[...]
    o_ref[...] = jnp.dot(_silu(c), w_ref[...], preferred_element_type=F32,
                         precision=lax.Precision.HIGHEST) + b_ref[...]


def _modulation(c, mod_w, mod_b, tn=1024):
    depth, d, n = mod_w.shape
    c8 = jnp.broadcast_to(c.reshape(1, d), (8, d))
    out = pl.pallas_call(
        _mod_kernel,
        grid=(depth, n // tn),
        in_specs=[pl.BlockSpec((8, d), lambda l, j: (0, 0)),
                  pl.BlockSpec((None, d, tn), lambda l, j: (l, 0, j)),
                  pl.BlockSpec((None, 1, tn), lambda l, j: (l, 0, j))],
        out_specs=pl.BlockSpec((None, 8, tn), lambda l, j: (l, 0, j)),
        out_shape=jax.ShapeDtypeStruct((depth, 8, n), F32),
        compiler_params=_params(("arbitrary", "arbitrary")),
        name="modulation",
    )(c8, mod_w, mod_b.reshape(depth, 1, n))
    return out[:, 0, :]


def _proj_kernel(x_ref, nw_ref, sc_ref, sh_ref, w_ref, wg_ref, o_ref, g_ref):
    h = _norm_mod(x_ref[...], nw_ref[...], sc_ref[...], sh_ref[...])
    hb = h.astype(BF16)
    o_ref[...] = jnp.dot(hb, w_ref[...], preferred_element_type=F32)
    ng = g_ref.shape[-1]

    @pl.when(pl.program_id(0) == 0)
    def _():
        r = jnp.dot(hb, wg_ref[...], preferred_element_type=F32)
        g_ref[...] = r[:, :ng] + r[:, ng:]

    @pl.when(pl.program_id(0) != 0)
    def _():
        g_ref[...] = jnp.zeros_like(g_ref)


def _norm_proj(x, nw, sc, sh, w, wg, tm, n_split=2):
    t, d = x.shape
    n = w.shape[1]
    ng = wg.shape[1]
    tn = n // n_split
    wg1 = wg.astype(BF16)
    wg2 = (wg - wg1.astype(F32)).astype(BF16)
    row = lambda s, i: (0, 0)
    out, gates = pl.pallas_call(
        _proj_kernel,
        grid=(n_split, t // tm),
        in_specs=[pl.BlockSpec((tm, d), lambda s, i: (i, 0)),
                  pl.BlockSpec((1, d), row), pl.BlockSpec((1, d), row), pl.BlockSpec((1, d), row),
                  pl.BlockSpec((d, tn), lambda s, i: (0, s), pipeline_mode=pl.Buffered(1)),
                  pl.BlockSpec((d, 2 * ng), row, pipeline_mode=pl.Buffered(1))],
        out_specs=[pl.BlockSpec((tm, tn), lambda s, i: (i, s)),
                   pl.BlockSpec((None, tm, ng), lambda s, i: (s, i, 0))],
        out_shape=[jax.ShapeDtypeStruct((t, n), F32), jax.ShapeDtypeStruct((n_split, t, ng), F32)],
        compiler_params=_params(("arbitrary", "arbitrary")),
        name="norm_proj",
    )(x, nw, sc, sh, w.astype(BF16), jnp.concatenate([wg1, wg2], axis=1))
    return out, gates[0]


def _out_kernel(n_in, *refs):
    y_refs = refs[:n_in]
    w_refs = refs[n_in:2 * n_in]
    x_ref, g_ref, nw_ref, sc_ref, sh_ref, o_ref, h_ref = refs[2 * n_in:]
    acc = jnp.dot(y_refs[0][...], w_refs[0][...], preferred_element_type=F32)
    for y_ref, w_ref in zip(y_refs[1:], w_refs[1:]):
        acc += jnp.dot(y_ref[...], w_ref[...], preferred_element_type=F32)
    x_new = x_ref[...] + g_ref[...] * acc
    o_ref[...] = x_new
    h_ref[...] = _norm_mod(x_new, nw_ref[...], sc_ref[...], sh_ref[...]).astype(h_ref.dtype)


def _out_proj(ys, ws, x, gate, nw, sc, sh, tm=512):
    t, d = x.shape
    n_in = len(ys)
    row = pl.BlockSpec((1, d), lambda i: (0, 0))
    in_specs = [pl.BlockSpec((tm, y.shape[1]), lambda i: (i, 0)) for y in ys]
    in_specs += [pl.BlockSpec(w.shape, lambda i: (0, 0), pipeline_mode=pl.Buffered(1)) for w in ws]
    in_specs += [pl.BlockSpec((tm, d), lambda i: (i, 0)), row, row, row, row]
    return pl.pallas_call(
        functools.partial(_out_kernel, n_in),
        grid=(t // tm,),
        in_specs=in_specs,
        out_specs=[pl.BlockSpec((tm, d), lambda i: (i, 0)), pl.BlockSpec((tm, d), lambda i: (i, 0))],
        out_shape=[jax.ShapeDtypeStruct((t, d), F32), jax.ShapeDtypeStruct((t, d), BF16)],
        compiler_params=_params(("arbitrary",)),
        name="out_proj",
    )(*ys, *[w.astype(BF16) for w in ws], x, gate, nw, sc, sh)


def _mlp_kernel(x_ref, h_ref, g_ref, w1_ref, w2_ref, o_ref):
    f = pl.program_id(1)

    @pl.when(f == 0)
    def _():
        o_ref[...] = jnp.zeros_like(o_ref)

    a = jnp.maximum(jnp.dot(h_ref[...], w1_ref[...], preferred_element_type=F32), 0.0)
    o_ref[...] += jnp.dot((a * a).astype(BF16), w2_ref[...], preferred_element_type=F32)

    @pl.when(f == pl.num_programs(1) - 1)
    def _():
        o_ref[...] = x_ref[...] + g_ref[...] * o_ref[...]


def _mlp(x, h, gate, w1, w2, tm=1024, tf=512):
    t, d = x.shape
    ff = w1.shape[1]
    return pl.pallas_call(
        _mlp_kernel,
        grid=(t // tm, ff // tf),
        in_specs=[pl.BlockSpec((tm, d), lambda i, f: (i, 0)),
                  pl.BlockSpec((tm, d), lambda i, f: (i, 0)),
                  pl.BlockSpec((1, d), lambda i, f: (0, 0)),
                  pl.BlockSpec((d, tf), lambda i, f: (0, f)),
                  pl.BlockSpec((tf, d), lambda i, f: (f, 0))],
        out_specs=pl.BlockSpec((tm, d), lambda i, f: (i, 0)),
        out_shape=jax.ShapeDtypeStruct((t, d), F32),
        compiler_params=_params(("parallel", "arbitrary")),
        name="mlp",
    )(x, h, gate, w1.astype(BF16), w2.astype(BF16))


ATT_TQ = 256
ATT_KB = 3
ATT_HG = 4


def _attn_kernel(q_ref, k_ref, v_ref, bias_ref, qw_ref, kw_ref, o_ref, kn_scr, v_scr):
    i = pl.program_id(0)

    @pl.when(i == 0)
    def _():
        kn_scr[...] = jnp.zeros_like(kn_scr)
        v_scr[...] = jnp.zeros_like(v_scr)

    def rms(x, w):
        return x * lax.rsqrt(jnp.mean(x * x, axis=-1, keepdims=True) + NORM_EPS) * w

    slots = [lax.rem(i + 1 + b, ATT_KB) for b in range(ATT_KB)]
    kw = kw_ref[...]
    qw = qw_ref[...]
    for h in range(ATT_HEADS):
        cols = slice(h * ATT_DH, (h + 1) * ATT_DH)
        kn_scr[slots[-1], :, cols] = rms(k_ref[:, cols], kw).astype(BF16)
    v_scr[slots[-1]] = v_ref[...].astype(BF16)

    for h0 in range(0, ATT_HEADS, ATT_HG):
        hs = range(h0, h0 + ATT_HG)
        col = {h: slice(h * ATT_DH, (h + 1) * ATT_DH) for h in hs}
        q = {h: (rms(q_ref[:, col[h]], qw) * (ATT_DH ** -0.5)).astype(BF16) for h in hs}
        s = {}
        for h in hs:
            for b in range(ATT_KB):
                sb = lax.dot_general(q[h], kn_scr[slots[b], :, col[h]], _NT, preferred_element_type=F32)
                sb = sb + bias_ref[h, :, b * ATT_TQ:(b + 1) * ATT_TQ]
                s[h, b] = jnp.where(i >= ATT_KB - 1 - b, sb, NEG_BIG)
        p, l = {}, {}
        for h in hs:
            m = jnp.max(jnp.maximum(jnp.maximum(s[h, 0], s[h, 1]), s[h, 2]), axis=-1, keepdims=True)
            for b in range(ATT_KB):
                p[h, b] = jnp.exp(s[h, b] - m)
            l[h] = jnp.sum(p[h, 0] + p[h, 1] + p[h, 2], axis=-1, keepdims=True)
        for h in hs:
            o = jnp.dot(p[h, 0].astype(BF16), v_scr[slots[0], :, col[h]], preferred_element_type=F32)
            for b in range(1, ATT_KB):
                o += jnp.dot(p[h, b].astype(BF16), v_scr[slots[b], :, col[h]], preferred_element_type=F32)
            o_ref[:, col[h]] = (o / l[h]).astype(o_ref.dtype)


def _attn_bias_table(rel_bias):
    nk = ATT_KB * ATT_TQ
    span = ATT_TQ + nk
    dist = jnp.arange(span) - (ATT_TQ - 1) - (ATT_KB - 1) * ATT_TQ
    ext = rel_bias.astype(F32)[:, jnp.clip(dist, -REL_CLIP, REL_CLIP) + REL_CLIP]
    flat = jnp.tile(ext, (1, ATT_TQ))[:, :ATT_TQ * (span - 1)]
    tab = flat.reshape(-1, ATT_TQ, span - 1)[:, :, ATT_TQ - 1:ATT_TQ - 1 + nk]
    r = jnp.arange(ATT_TQ)[:, None]
    m = jnp.arange(nk)[None, :]
    qc = r // CHUNK
    kc = m // CHUNK
    in_band = (kc >= qc) & (kc <= qc + BAND_CHUNKS - 1)
    return jnp.where(in_band[None], tab, NEG_BIG)


def _band_attention(proj, q_blk, k_blk, v_blk, q_norm_w, k_norm_w, rel_bias):
    t = proj.shape[0]
    bias = _attn_bias_table(rel_bias)
    blk = (ATT_TQ, ATT_W)
    return pl.pallas_call(
        _attn_kernel,
        grid=(t // ATT_TQ,),
        in_specs=[pl.BlockSpec(blk, lambda i: (i, q_blk)),
                  pl.BlockSpec(blk, lambda i: (i, k_blk)),
                  pl.BlockSpec(blk, lambda i: (i, v_blk)),
                  pl.BlockSpec(bias.shape, lambda i: (0, 0, 0), pipeline_mode=pl.Buffered(1)),
                  pl.BlockSpec((1, ATT_DH), lambda i: (0, 0)),
                  pl.BlockSpec((1, ATT_DH), lambda i: (0, 0))],
        out_specs=pl.BlockSpec(blk, lambda i: (i, 0)),
        out_shape=jax.ShapeDtypeStruct((t, ATT_W), BF16),
        scratch_shapes=[pltpu.VMEM((ATT_KB, ATT_TQ, ATT_W), BF16), pltpu.VMEM((ATT_KB, ATT_TQ, ATT_W), BF16)],
        compiler_params=_params(("arbitrary",)),
        name="band_attention",
    )(proj, proj, proj, bias, q_norm_w.reshape(1, ATT_DH), k_norm_w.reshape(1, ATT_DH))


GDN_TB = 1024
GDN_NC = GDN_TB // CHUNK
GDN_HPS = 4
GDN_GRP = 4


def _causal_conv(cur_ref, prev_ref, w_ref, scr, first_block):
    rows = cur_ref.shape[0]
    prev = prev_ref[...]
    scr[0:HALO, :] = jnp.where(first_block, jnp.zeros_like(prev), prev)
    scr[HALO:HALO + rows, :] = cur_ref[...]
    w = w_ref[...]
    acc = scr[pl.ds(HALO - (CONV_K - 1), rows), :] * w[0:1, :]
    for j in range(1, CONV_K):
        acc += scr[pl.ds(HALO - (CONV_K - 1) + j, rows), :] * w[j:j + 1, :]
    return acc


def _block_diag(h):
    blk = lax.broadcasted_iota(jnp.int32, (1, h.shape[1]), 1) // CHUNK
    return jnp.concatenate([jnp.where(blk == j, h, jnp.zeros_like(h)) for j in range(h.shape[1] // CHUNK)], axis=0)


def _gdn_kernel(qc_ref, qp_ref, kc_ref, kp_ref, vc_ref, vp_ref, wq_ref, wk_ref, wv_ref,
                gate_ref, alog_ref, dtb_ref, z_ref, nw_ref, o_ref,
                cq_scr, ck_scr, cv_scr, beta_scr, gcum_scr, gt_scr,
                wq_scr, u_scr, qk_scr, a_scr, b_scr, s_scr):
    i = pl.program_id(0)
    hp = pl.program_id(1)
    tb = GDN_TB
    first = i == 0

    @pl.when(first)
    def _():
        for hl in range(GDN_HPS):
            s_scr[hp * GDN_HPS + hl] = jnp.zeros(s_scr.shape[1:], F32)

    @pl.when(hp == 0)
    def _():
        gate = gate_ref[...]
        beta_scr[...] = jax.nn.sigmoid(gate)
        g = -jnp.exp(alog_ref[...]) * _softplus(gate + dtb_ref[...])
        gcum = _dot_exact_lhs(_seg_tri(tb), g)
        gcum_scr[...] = gcum
        for s in range(tb // LANES):
            gt_scr[s] = gcum[s * LANES:(s + 1) * LANES, :].T

    lane = lax.broadcasted_iota(jnp.int32, (1, LANES), 1)
    r = lax.broadcasted_iota(jnp.int32, (CHUNK, CHUNK), 0)
    c = lax.broadcasted_iota(jnp.int32, (CHUNK, CHUNK), 1)
    causal = r >= c
    strict = r > c
    rh = lax.broadcasted_iota(jnp.int32, (CHUNK, GDN_GRP * CHUNK), 0)
    ch = lax.broadcasted_iota(jnp.int32, (CHUNK, GDN_GRP * CHUNK), 1)
    eye_h = jnp.where(rh == ch % CHUNK, 1.0, 0.0)
    nw = nw_ref[...]
    heads = range(GDN_HPS)
    groups = range(GDN_NC // GDN_GRP)
    gw = GDN_GRP * CHUNK

    def l2n(x):
        return x * lax.rsqrt(jnp.sum(x * x, axis=-1, keepdims=True) + NORM_EPS)

    qa = _silu(_causal_conv(qc_ref, qp_ref, wq_ref, cq_scr, first))
    ka = _silu(_causal_conv(kc_ref, kp_ref, wk_ref, ck_scr, first))
    va = _silu(_causal_conv(vc_ref, vp_ref, wv_ref, cv_scr, first))

    gcs, ks, rhs, low_hs, p_hs, x_hs = {}, {}, {}, {}, {}, {}
    for hl in heads:
        h = hp * GDN_HPS + hl
        cols = slice(hl * LANES, (hl + 1) * LANES)
        beta = jnp.sum(jnp.where(lane == h, beta_scr[...], 0.0), axis=1, keepdims=True)
        gc = jnp.sum(jnp.where(lane == GDN_HEADS + h, gcum_scr[...], 0.0), axis=1, keepdims=True)
        q = l2n(qa[:, cols]) * (GDN_DK ** -0.5)
        k = l2n(ka[:, cols])
        eg = jnp.exp(gc)
        kb = k * beta
        rhs[hl] = jnp.concatenate([va[:, cols] * beta, kb * eg], axis=1).astype(BF16)
        qd = (q * eg).astype(BF16)
        gcs[hl], ks[hl] = gc, k
        for grp in groups:
            lows = []
            for j in range(GDN_GRP):
                n = grp * GDN_GRP + j
                rows = slice(n * CHUNK, (n + 1) * CHUNK)
                gc_n = gc[rows]
                grow = gt_scr[n // 2, pl.ds(GDN_HEADS + h, 1), :]
                grow = grow[:, (n % 2) * CHUNK:(n % 2 + 1) * CHUNK]
                decay = jnp.where(causal, jnp.exp(jnp.where(causal, gc_n - grow, 0.0)), 0.0)
                ab = _dot_nt(jnp.concatenate([kb[rows], q[rows]], axis=0), k[rows])
                lows.append(jnp.where(strict, ab[:CHUNK] * decay, 0.0))
                qk_scr[hl, rows, :] = (ab[CHUNK:] * decay).astype(BF16)
                wq_scr[hl, n, CHUNK:2 * CHUNK, :] = qd[rows]
            low_hs[hl, grp] = jnp.concatenate(lows, axis=1)

    for key, low_h in low_hs.items():
        xb = low_h.astype(BF16)
        p_hs[key] = eye_h - low_h
        x_hs[key] = jnp.dot(xb, _block_diag(xb), preferred_element_type=F32)
    for step in range(5):
        for key in low_hs:
            xb = x_hs[key].astype(BF16)
            x_bd = _block_diag(xb)
            p_hs[key] = p_hs[key] + jnp.dot(p_hs[key].astype(BF16), x_bd, preferred_element_type=F32)
            if step < 4:
                x_hs[key] = jnp.dot(xb, x_bd, preferred_element_type=F32)

    for (hl, grp), p_h in p_hs.items():
        g0 = grp * gw
        uw = jnp.dot(_block_diag(p_h.astype(BF16)), rhs[hl][g0:g0 + gw], preferred_element_type=F32)
        u_scr[hl, g0:g0 + gw, :] = uw[:, :GDN_DV]
        for j in range(GDN_GRP):
            n = grp * GDN_GRP + j
            rows = slice(n * CHUNK, (n + 1) * CHUNK)
            uw_n = uw[j * CHUNK:(j + 1) * CHUNK]
            wq_scr[hl, n, 0:CHUNK, :] = uw_n[:, GDN_DV:].astype(BF16)
            gc_n = gcs[hl][rows]
            kd = ks[hl][rows] * jnp.exp(gc_n[CHUNK - 1:CHUNK, :] - gc_n)
            ba = _dot_tn(kd, uw_n)
            b_scr[hl, n] = ba[:, :GDN_DV]
            a_scr[hl, n] = ba[:, GDN_DV:].astype(BF16)

    states = [s_scr[hp * GDN_HPS + hl] for hl in heads]
    for n in range(GDN_NC):
        rows = slice(n * CHUNK, (n + 1) * CHUNK)
        for hl in heads:
            cols = slice(hl * LANES, (hl + 1) * LANES)
            sb = states[hl].astype(BF16)
            ws = jnp.dot(wq_scr[hl, n], sb, preferred_element_type=F32)
            g_last = jnp.exp(gcs[hl][(n + 1) * CHUNK - 1:(n + 1) * CHUNK, :])
            states[hl] = states[hl] * g_last - jnp.dot(a_scr[hl, n], sb, preferred_element_type=F32) + b_scr[hl, n]
            v_new = u_scr[hl, rows, :] - ws[:CHUNK]
            o = ws[CHUNK:] + jnp.dot(qk_scr[hl, rows, :], v_new.astype(BF16), preferred_element_type=F32)
            on = o * lax.rsqrt(jnp.mean(o * o, axis=-1, keepdims=True) + NORM_EPS) * nw
            o_ref[rows, cols] = (on * _silu(z_ref[rows, cols])).astype(o_ref.dtype)
    for hl in heads:
        s_scr[hp * GDN_HPS + hl] = states[hl]


def _gated_deltanet(proj, gates, conv_w, a_log, dt_bias, norm_w):
    t = proj.shape[0]
    tb = GDN_TB
    nh = GDN_HEADS
    w = GDN_HPS * LANES
    nb = GDN_QK // w
    pad = lambda a: jnp.zeros((1, LANES), F32).at[0, nh:2 * nh].set(a.astype(F32))
    cur = lambda sec: pl.BlockSpec((tb, w), lambda i, h: (i, sec * nb + h))
    prev = lambda sec: pl.BlockSpec((HALO, w), lambda i, h: (jnp.maximum(i * (tb // HALO) - 1, 0), sec * nb + h))
    cw = lambda sec: pl.BlockSpec((CONV_K, w), lambda i, h: (0, sec * nb + h))
    row = pl.BlockSpec((1, LANES), lambda i, h: (0, 0))
    return pl.pallas_call(
        _gdn_kernel,
        grid=(t // tb, nh // GDN_HPS),
        in_specs=[cur(0), prev(0), cur(1), prev(1), cur(2), prev(2), cw(0), cw(1), cw(2),
                  pl.BlockSpec((tb, LANES), lambda i, h: (i, 0)), row, row, cur(3), row],
        out_specs=pl.BlockSpec((tb, w), lambda i, h: (i, h)),
        out_shape=jax.ShapeDtypeStruct((t, GDN_V), BF16),
        scratch_shapes=[pltpu.VMEM((tb + HALO, w), F32)] * 3 + [
            pltpu.VMEM((tb, LANES), F32), pltpu.VMEM((tb, LANES), F32),
            pltpu.VMEM((tb // LANES, LANES, LANES), F32),
            pltpu.VMEM((GDN_HPS, GDN_NC, 2 * CHUNK, GDN_DK), BF16),
            pltpu.VMEM((GDN_HPS, tb, GDN_DV), F32),
            pltpu.VMEM((GDN_HPS, tb, CHUNK), BF16),
            pltpu.VMEM((GDN_HPS, GDN_NC, GDN_DK, GDN_DK), BF16),
            pltpu.VMEM((GDN_HPS, GDN_NC, GDN_DK, GDN_DV), F32),
            pltpu.VMEM((nh, GDN_DK, GDN_DV), F32)],
        compiler_params=_params(("arbitrary", "arbitrary")),
        name="gated_deltanet",
    )(proj, proj, proj, proj, proj, proj, conv_w, conv_w, conv_w,
      gates, pad(a_log), pad(dt_bias), proj, norm_w.reshape(1, GDN_DV))


SSD_TB = 512
SSD_NC = SSD_TB // CHUNK


def _ssd_expand_matrix():
    j = jnp.arange(LANES)[:, None]
    col = jnp.arange(2 * SSD_GW)[None, :]
    head = (col % SSD_GW) // SSD_HEADDIM
    piece = j // SSD_HPG
    is_ac = (col < SSD_GW) & (piece < 3)
    is_dt = (col >= SSD_GW) & (piece >= 3) & (piece < 5)
    return jnp.where((is_ac | is_dt) & (j % SSD_HPG == head), 1.0, 0.0).astype(BF16)


def _ssd_kernel(xc_ref, xp_ref, bc_ref, bp_ref, cc_ref, cp_ref, wx_ref, wb_ref, wc_ref,
                bx_ref, bb_ref, bcn_ref, dt_ref, dtb_ref, alog_ref, dsk_ref, z_ref, nw_ref, ex_ref, o_ref,
                cx_scr, cb_scr, cc_scr, dt_scr, ac_scr, x_scr, e_scr, st_scr):
    i = pl.program_id(0)
    g = pl.program_id(1)
    tb = SSD_TB
    hpg = SSD_HPG
    first = i == 0

    @pl.when(first)
    def _():
        st_scr[g] = jnp.zeros(st_scr.shape[1:], F32)

    @pl.when(g == 0)
    def _():
        dt = _softplus(dt_ref[...] + dtb_ref[...])
        dt_scr[...] = dt
        ac_scr[...] = _dot_exact_lhs(_seg_tri(tb), dt * (-jnp.exp(alog_ref[...])))

    shift = (LANES - g * hpg) % LANES
    lane = lax.broadcasted_iota(jnp.int32, (1, LANES), 1)
    mine = lane < hpg
    dt_g = jnp.where(mine, pltpu.roll(dt_scr[...], shift, 1), 0.0)
    ac_g = jnp.where(mine, pltpu.roll(ac_scr[...], shift, 1), 0.0)

    a1, a2, a3 = _split3(ac_g)
    d1 = dt_g.astype(BF16)
    d2 = (dt_g - d1.astype(F32)).astype(BF16)
    packed = a1.astype(F32)
    for k, piece in enumerate((a2, a3, d1, d2), start=1):
        packed = packed + pltpu.roll(piece.astype(F32), k * hpg, 1)
    e_scr[...] = jnp.dot(packed.astype(BF16), ex_ref[...], preferred_element_type=F32)

    ac_t = [ac_g[s * LANES:(s + 1) * LANES, :].T[0:8, :] for s in range(tb // LANES)]
    ac_tr = [pltpu.roll(a, CHUNK, 1) for a in ac_t]

    x_scr[...] = _silu(_causal_conv(xc_ref, xp_ref, wx_ref, cx_scr, first) + bx_ref[...])
    bm = _silu(_causal_conv(bc_ref, bp_ref, wb_ref, cb_scr, first) + bb_ref[...])
    cm = _silu(_causal_conv(cc_ref, cp_ref, wc_ref, cc_scr, first) + bcn_ref[...])

    r = lax.broadcasted_iota(jnp.int32, (CHUNK, LANES), 0)
    c = lax.broadcasted_iota(jnp.int32, (CHUNK, LANES), 1)
    causal2 = r >= (c % CHUNK)
    left = c < CHUNK
    dsk = dsk_ref[...]
    nw = nw_ref[...]
    gw = SSD_GW

    for n in range(SSD_NC):
        rows = slice(n * CHUNK, (n + 1) * CHUNK)
        x_n, b_n, c_n = x_scr[rows, :], bm[rows], cm[rows]
        acx = e_scr[rows, 0:gw]
        xdt = x_n * e_scr[rows, gw:2 * gw]
        a_last = acx[CHUNK - 1:CHUNK, :]
        state = st_scr[g]
        cb = _dot_nt(c_n, b_n)
        cb2 = jnp.concatenate([cb, cb], axis=1)
        y_off = _dot(c_n, state)
        t_lo, t_hi = (ac_t[n // 2], ac_tr[n // 2]) if n % 2 == 0 else (ac_tr[n // 2], ac_t[n // 2])
        ys = []
        for p in range(hpg // 2):
            lanes = slice(p * LANES, (p + 1) * LANES)
            a_row = jnp.where(lane < CHUNK, t_lo[2 * p:2 * p + 1, :], t_hi[2 * p + 1:2 * p + 2, :])
            decay = jnp.where(causal2, jnp.exp(jnp.where(causal2, acx[:, lanes] - a_row, 0.0)), 0.0)
            xp = xdt[:, lanes]
            x_bd = jnp.concatenate([jnp.where(left, xp, 0.0), jnp.where(left, 0.0, xp)], axis=0)
            ys.append(_dot(cb2 * decay, x_bd))
        y = jnp.concatenate(ys, axis=1) + y_off * jnp.exp(acx) + x_n * dsk
        st_scr[g] = state * jnp.exp(a_last) + _dot_tn(b_n, xdt * jnp.exp(a_last - acx))
        y = y * _silu(z_ref[rows, :])
        y = y * lax.rsqrt(jnp.mean(y * y, axis=-1, keepdims=True) + NORM_EPS) * nw
        o_ref[rows, :] = y.astype(o_ref.dtype)


def _mamba2_ssd(proj, dt_raw, conv_w, conv_b, dt_bias, a_log, d_skip, norm_w):
    t = proj.shape[0]
    tb = SSD_TB
    gw = SSD_GW
    ng = SSD_GROUPS
    x0 = SSD_DINNER // gw
    b0 = 2 * SSD_DINNER // LANES
    c0 = b0 + ng
    pad = lambda a: jnp.zeros((1, LANES), F32).at[0, :SSD_HEADS].set(a.astype(F32))
    prev_idx = lambda i: jnp.maximum(i * (tb // HALO) - 1, 0)
    cbias = conv_b.reshape(1, -1)
    dsk = jnp.repeat(d_skip.astype(F32), SSD_HEADDIM).reshape(1, SSD_DINNER)
    return pl.pallas_call(
        _ssd_kernel,
        grid=(t // tb, ng),
        in_specs=[pl.BlockSpec((tb, gw), lambda i, g: (i, x0 + g)),
                  pl.BlockSpec((HALO, gw), lambda i, g: (prev_idx(i), x0 + g)),
                  pl.BlockSpec((tb, LANES), lambda i, g: (i, b0 + g)),
                  pl.BlockSpec((HALO, LANES), lambda i, g: (prev_idx(i), b0 + g)),
                  pl.BlockSpec((tb, LANES), lambda i, g: (i, c0 + g)),
                  pl.BlockSpec((HALO, LANES), lambda i, g: (prev_idx(i), c0 + g)),
                  pl.BlockSpec((CONV_K, gw), lambda i, g: (0, g)),
                  pl.BlockSpec((CONV_K, LANES), lambda i, g: (0, SSD_DINNER // LANES + g)),
                  pl.BlockSpec((CONV_K, LANES), lambda i, g: (0, SSD_DINNER // LANES + ng + g)),
                  pl.BlockSpec((1, gw), lambda i, g: (0, g)),
                  pl.BlockSpec((1, LANES), lambda i, g: (0, SSD_DINNER // LANES + g)),
                  pl.BlockSpec((1, LANES), lambda i, g: (0, SSD_DINNER // LANES + ng + g)),
                  pl.BlockSpec((tb, LANES), lambda i, g: (i, 0)),
                  pl.BlockSpec((1, LANES), lambda i, g: (0, 0)),
                  pl.BlockSpec((1, LANES), lambda i, g: (0, 0)),
                  pl.BlockSpec((1, gw), lambda i, g: (0, g)),
                  pl.BlockSpec((tb, gw), lambda i, g: (i, g)),
                  pl.BlockSpec((1, gw), lambda i, g: (0, g)),
                  pl.BlockSpec((LANES, 2 * gw), lambda i, g: (0, 0))],
        out_specs=pl.BlockSpec((tb, gw), lambda i, g: (i, g)),
        out_shape=jax.ShapeDtypeStruct((t, SSD_DINNER), BF16),
        scratch_shapes=[pltpu.VMEM((tb + HALO, gw), F32), pltpu.VMEM((tb + HALO, LANES), F32),
                        pltpu.VMEM((tb + HALO, LANES), F32),
                        pltpu.VMEM((tb, LANES), F32), pltpu.VMEM((tb, LANES), F32),
                        pltpu.VMEM((tb, gw), F32), pltpu.VMEM((tb, 2 * gw), F32),
                        pltpu.VMEM((ng, SSD_DSTATE, gw), F32)],
        compiler_params=_params(("arbitrary", "arbitrary")),
        name="mamba2_ssd",
    )(proj, proj, proj, proj, proj, proj, conv_w, conv_w, conv_w, cbias, cbias, cbias,
      dt_raw, pad(dt_bias), pad(a_log), dsk, proj, norm_w.reshape(1, SSD_DINNER), _ssd_expand_matrix())


def _layer_mods(mod, layer):
    d = D_MODEL
    return [mod[layer, k * d:(k + 1) * d].reshape(1, d) for k in range(6)]


def kernel(x, c, mod_w, mod_b, norm_mix_w, norm_mlp_w, mlp_w1, mlp_w2, ab_w_in, gdn_conv_w, gdn_a_log,
           gdn_dt_bias, gdn_norm_w, attn_q_norm_w, attn_k_norm_w, attn_rel_bias, ab_w_out, ssd_w_in,
           ssd_conv_w, ssd_conv_b, ssd_dt_bias, ssd_a_log, ssd_d, ssd_norm_w, ssd_w_out):
    b, t, d = x.shape
    assert b == 1 and d == D_MODEL
    xs = x.reshape(t, d)
    mod = _modulation(c, mod_w, mod_b)

    sh1, sc1, g1, sh2, sc2, g2 = _layer_mods(mod, 0)
    w_in = ab_w_in[0]
    gate_lo = 2 * GDN_QK + 2 * GDN_V
    gate_hi = gate_lo + 2 * GDN_HEADS
    w_in_b = w_in.astype(BF16)
    w_main = jnp.concatenate([w_in_b[:, :gate_lo], w_in_b[:, gate_hi:]], axis=1)
    w_gate = jnp.pad(w_in[:, gate_lo:gate_hi], ((0, 0), (0, LANES - 2 * GDN_HEADS)))
    proj, gates = _norm_proj(xs, norm_mix_w[0].reshape(1, d), sc1, sh1, w_main, w_gate, tm=512)
    o_a = _gated_deltanet(proj, gates, gdn_conv_w[0], gdn_a_log[0], gdn_dt_bias[0], gdn_norm_w[0])
    a0 = gate_lo // ATT_W
    o_b = _band_attention(proj, a0, a0 + 1, a0 + 2, attn_q_norm_w[0], attn_k_norm_w[0], attn_rel_bias[0])
    xs, h2 = _out_proj([o_a, o_b], [ab_w_out[0][:GDN_V], ab_w_out[0][GDN_V:]], xs, g1,
                       norm_mlp_w[0].reshape(1, d), sc2, sh2)
    xs = _mlp(xs, h2, g2, mlp_w1[0], mlp_w2[0])

    sh1, sc1, g1, sh2, sc2, g2 = _layer_mods(mod, 1)
    w_in = ssd_w_in[0]
    n_main = 2 * SSD_DINNER + 2 * SSD_GROUPS * SSD_DSTATE
    w_dt = jnp.pad(w_in[:, n_main:], ((0, 0), (0, LANES - SSD_HEADS)))
    proj, dt_raw = _norm_proj(xs, norm_mix_w[1].reshape(1, d), sc1, sh1, w_in[:, :n_main], w_dt, tm=256)
    y = _mamba2_ssd(proj, dt_raw, ssd_conv_w[0], ssd_conv_b[0], ssd_dt_bias[0], ssd_a_log[0],
                    ssd_d[0], ssd_norm_w[0])
    xs, h2 = _out_proj([y], [ssd_w_out[0]], xs, g1, norm_mlp_w[1].reshape(1, d), sc2, sh2)
    xs = _mlp(xs, h2, g2, mlp_w1[1], mlp_w2[1])
    return xs.reshape(b, t, d)
```

```python
import functools

import jax
import jax.numpy as jnp
from jax import lax
from jax.experimental import pallas as pl
from jax.experimental.pallas import tpu as pltpu

F32 = jnp.float32
BF16 = jnp.bfloat16

D_MODEL = 2048
CHUNK = 64
NORM_EPS = 1e-6
CONV_K = 4
HALO = 8
GDN_HEADS = 8
GDN_DK = 128
GDN_DV = 128
GDN_QK = GDN_HEADS * GDN_DK
GDN_V = GDN_HEADS * GDN_DV
ATT_HEADS = 8
ATT_DH = 128
ATT_W = ATT_HEADS * ATT_DH
BAND_CHUNKS = 9
REL_CLIP = 256
SSD_DINNER = 2 * D_MODEL
SSD_HEADDIM = 64
SSD_HEADS = SSD_DINNER // SSD_HEADDIM
SSD_GROUPS = 8
SSD_HPG = SSD_HEADS // SSD_GROUPS
SSD_DSTATE = 128
SSD_GW = SSD_HPG * SSD_HEADDIM
D_FF = 4 * D_MODEL
LANES = 128
NEG_BIG = -1e30

VMEM_LIMIT = 56 * 1024 * 1024

_NT = (((1,), (1,)), ((), ()))
_TN = (((0,), (0,)), ((), ()))


def _dot(a, b):
    return jnp.dot(a.astype(BF16), b.astype(BF16), preferred_element_type=F32)


def _dot_nt(a, b):
    return lax.dot_general(a.astype(BF16), b.astype(BF16), _NT, preferred_element_type=F32)


def _dot_tn(a, b):
    return lax.dot_general(a.astype(BF16), b.astype(BF16), _TN, preferred_element_type=F32)


def _split3(a):
    a1 = a.astype(BF16)
    r1 = a - a1.astype(F32)
    a2 = r1.astype(BF16)
    a3 = (r1 - a2.astype(F32)).astype(BF16)
    return a1, a2, a3


def _dot_exact_lhs(a_bf16, b):
    b1, b2, b3 = _split3(b)
    out = jnp.dot(a_bf16, b1, preferred_element_type=F32)
    out += jnp.dot(a_bf16, b2, preferred_element_type=F32)
    out += jnp.dot(a_bf16, b3, preferred_element_type=F32)
    return out


def _silu(x):
    return x * jax.nn.sigmoid(x)


def _softplus(x):
    return jnp.maximum(x, 0.0) + jnp.log(1.0 + jnp.exp(-jnp.abs(x)))


def _norm_mod(x, nw, sc, sh):
    ms = jnp.mean(x * x, axis=-1, keepdims=True)
    return (x * lax.rsqrt(ms + NORM_EPS) * nw) * (1.0 + sc) + sh


def _seg_tri(n):
    r = lax.broadcasted_iota(jnp.int32, (n, n), 0)
    c = lax.broadcasted_iota(jnp.int32, (n, n), 1)
    return jnp.where((r >= c) & ((r // CHUNK) == (c // CHUNK)), 1.0, 0.0).astype(BF16)


MXU_K = 256


def _chunk_cumsum(g):
    tri = _seg_tri(MXU_K)
    return jnp.concatenate([_dot_exact_lhs(tri, g[s:s + MXU_K]) for s in range(0, g.shape[0], MXU_K)], axis=0)


def _params(sem):
    return pltpu.CompilerParams(dimension_semantics=sem, vmem_limit_bytes=VMEM_LIMIT)


def _mod_kernel(c_ref, w_ref, b_ref, o_ref):
    c = c_ref[...]
    o_ref[...] = jnp.dot(_silu(c), w_ref[...], preferred_element_type=F32,
                         precision=lax.Precision.HIGHEST) + b_ref[...]


def _modulation(c, mod_w, mod_b, tn=1024):
    depth, d, n = mod_w.shape
    c8 = jnp.broadcast_to(c.reshape(1, d), (8, d))
    out = pl.pallas_call(
        _mod_kernel,
        grid=(depth, n // tn),
        in_specs=[pl.BlockSpec((8, d), lambda l, j: (0, 0)),
                  pl.BlockSpec((None, d, tn), lambda l, j: (l, 0, j)),
                  pl.BlockSpec((None, 1, tn), lambda l, j: (l, 0, j))],
        out_specs=pl.BlockSpec((None, 8, tn), lambda l, j: (l, 0, j)),
        out_shape=jax.ShapeDtypeStruct((depth, 8, n), F32),
        compiler_params=_params(("arbitrary", "arbitrary")),
        name="modulation",
    )(c8, mod_w, mod_b.reshape(depth, 1, n))
    return out[:, 0, :]


def _proj_kernel(x_ref, nw_ref, sc_ref, sh_ref, w_ref, wg_ref, o_ref, g_ref):
    h = _norm_mod(x_ref[...], nw_ref[...], sc_ref[...], sh_ref[...])
    hb = h.astype(BF16)
    o_ref[...] = jnp.dot(hb, w_ref[...], preferred_element_type=F32)
    ng = g_ref.shape[-1]

    @pl.when(pl.program_id(0) == 0)
    def _():
        r = jnp.dot(hb, wg_ref[...], preferred_element_type=F32)
        g_ref[...] = r[:, :ng] + r[:, ng:]

    @pl.when(pl.program_id(0) != 0)
    def _():
        g_ref[...] = jnp.zeros_like(g_ref)


def _norm_proj(x, nw, sc, sh, w, wg, tm, n_split=2):
    t, d = x.shape
    n = w.shape[1]
    ng = wg.shape[1]
    tn = n // n_split
    wg1 = wg.astype(BF16)
    wg2 = (wg - wg1.astype(F32)).astype(BF16)
    row = lambda s, i: (0, 0)
    out, gates = pl.pallas_call(
        _proj_kernel,
        grid=(n_split, t // tm),
        in_specs=[pl.BlockSpec((tm, d), lambda s, i: (i, 0)),
                  pl.BlockSpec((1, d), row), pl.BlockSpec((1, d), row), pl.BlockSpec((1, d), row),
                  pl.BlockSpec((d, tn), lambda s, i: (0, s), pipeline_mode=pl.Buffered(1)),
                  pl.BlockSpec((d, 2 * ng), row, pipeline_mode=pl.Buffered(1))],
        out_specs=[pl.BlockSpec((tm, tn), lambda s, i: (i, s)),
                   pl.BlockSpec((None, tm, ng), lambda s, i: (s, i, 0))],
        out_shape=[jax.ShapeDtypeStruct((t, n), F32), jax.ShapeDtypeStruct((n_split, t, ng), F32)],
        compiler_params=_params(("arbitrary", "arbitrary")),
        name="norm_proj",
    )(x, nw, sc, sh, w.astype(BF16), jnp.concatenate([wg1, wg2], axis=1))
    return out, gates[0]


def _out_kernel(n_in, *refs):
    y_refs = refs[:n_in]
    w_refs = refs[n_in:2 * n_in]
    x_ref, g_ref, nw_ref, sc_ref, sh_ref, o_ref, h_ref = refs[2 * n_in:]
    acc = jnp.dot(y_refs[0][...], w_refs[0][...], preferred_element_type=F32)
    for y_ref, w_ref in zip(y_refs[1:], w_refs[1:]):
        acc += jnp.dot(y_ref[...], w_ref[...], preferred_element_type=F32)
    x_new = x_ref[...] + g_ref[...] * acc
    o_ref[...] = x_new
    h_ref[...] = _norm_mod(x_new, nw_ref[...], sc_ref[...], sh_ref[...]).astype(h_ref.dtype)


def _out_proj(ys, ws, x, gate, nw, sc, sh, tm=512):
    t, d = x.shape
    n_in = len(ys)
    row = pl.BlockSpec((1, d), lambda i: (0, 0))
    in_specs = [pl.BlockSpec((tm, y.shape[1]), lambda i: (i, 0)) for y in ys]
    in_specs += [pl.BlockSpec(w.shape, lambda i: (0, 0), pipeline_mode=pl.Buffered(1)) for w in ws]
    in_specs += [pl.BlockSpec((tm, d), lambda i: (i, 0)), row, row, row, row]
    return pl.pallas_call(
        functools.partial(_out_kernel, n_in),
        grid=(t // tm,),
        in_specs=in_specs,
        out_specs=[pl.BlockSpec((tm, d), lambda i: (i, 0)), pl.BlockSpec((tm, d), lambda i: (i, 0))],
        out_shape=[jax.ShapeDtypeStruct((t, d), F32), jax.ShapeDtypeStruct((t, d), BF16)],
        compiler_params=_params(("arbitrary",)),
        name="out_proj",
    )(*ys, *[w.astype(BF16) for w in ws], x, gate, nw, sc, sh)


def _mlp_kernel(x_ref, h_ref, g_ref, w1_ref, w2_ref, o_ref):
    f = pl.program_id(1)

    @pl.when(f == 0)
    def _():
        o_ref[...] = jnp.zeros_like(o_ref)

    a = jnp.maximum(jnp.dot(h_ref[...], w1_ref[...], preferred_element_type=F32), 0.0)
    o_ref[...] += jnp.dot((a * a).astype(BF16), w2_ref[...], preferred_element_type=F32)

    @pl.when(f == pl.num_programs(1) - 1)
    def _():
        o_ref[...] = x_ref[...] + g_ref[...] * o_ref[...]


def _mlp(x, h, gate, w1, w2, tm=1024, tf=512):
    t, d = x.shape
    ff = w1.shape[1]
    return pl.pallas_call(
        _mlp_kernel,
        grid=(t // tm, ff // tf),
        in_specs=[pl.BlockSpec((tm, d), lambda i, f: (i, 0)),
                  pl.BlockSpec((tm, d), lambda i, f: (i, 0)),
                  pl.BlockSpec((1, d), lambda i, f: (0, 0)),
                  pl.BlockSpec((d, tf), lambda i, f: (0, f)),
                  pl.BlockSpec((tf, d), lambda i, f: (f, 0))],
        out_specs=pl.BlockSpec((tm, d), lambda i, f: (i, 0)),
        out_shape=jax.ShapeDtypeStruct((t, d), F32),
        compiler_params=_params(("parallel", "arbitrary")),
        name="mlp",
    )(x, h, gate, w1.astype(BF16), w2.astype(BF16))


ATT_TQ = 256
ATT_KB = 3
ATT_HG = 4


def _attn_kernel(q_ref, k_ref, v_ref, bias_ref, qw_ref, kw_ref, o_ref, kn_scr, v_scr):
    i = pl.program_id(0)

    @pl.when(i == 0)
    def _():
        kn_scr[...] = jnp.zeros_like(kn_scr)
        v_scr[...] = jnp.zeros_like(v_scr)

    def rms(x, w):
        return x * lax.rsqrt(jnp.mean(x * x, axis=-1, keepdims=True) + NORM_EPS) * w

    slots = [lax.rem(i + 1 + b, ATT_KB) for b in range(ATT_KB)]
    kw = kw_ref[...]
    qw = qw_ref[...]
    for h in range(ATT_HEADS):
        cols = slice(h * ATT_DH, (h + 1) * ATT_DH)
        kn_scr[slots[-1], :, cols] = rms(k_ref[:, cols], kw).astype(BF16)
    v_scr[slots[-1]] = v_ref[...].astype(BF16)

    for h0 in range(0, ATT_HEADS, ATT_HG):
        hs = range(h0, h0 + ATT_HG)
        col = {h: slice(h * ATT_DH, (h + 1) * ATT_DH) for h in hs}
        q = {h: (rms(q_ref[:, col[h]], qw) * (ATT_DH ** -0.5)).astype(BF16) for h in hs}
        s = {}
        for h in hs:
            for b in range(ATT_KB):
                sb = lax.dot_general(q[h], kn_scr[slots[b], :, col[h]], _NT, preferred_element_type=F32)
                sb = sb + bias_ref[h, :, b * ATT_TQ:(b + 1) * ATT_TQ]
                s[h, b] = jnp.where(i >= ATT_KB - 1 - b, sb, NEG_BIG)
        p, l = {}, {}
        for h in hs:
            m = jnp.max(jnp.maximum(jnp.maximum(s[h, 0], s[h, 1]), s[h, 2]), axis=-1, keepdims=True)
            for b in range(ATT_KB):
                p[h, b] = jnp.exp(s[h, b] - m)
            l[h] = jnp.sum(p[h, 0] + p[h, 1] + p[h, 2], axis=-1, keepdims=True)
        for h in hs:
            o = jnp.dot(p[h, 0].astype(BF16), v_scr[slots[0], :, col[h]], preferred_element_type=F32)
            for b in range(1, ATT_KB):
                o += jnp.dot(p[h, b].astype(BF16), v_scr[slots[b], :, col[h]], preferred_element_type=F32)
            o_ref[:, col[h]] = (o / l[h]).astype(o_ref.dtype)


def _attn_bias_table(rel_bias):
    nk = ATT_KB * ATT_TQ
    span = ATT_TQ + nk
    dist = jnp.arange(span) - (ATT_TQ - 1) - (ATT_KB - 1) * ATT_TQ
    ext = rel_bias.astype(F32)[:, jnp.clip(dist, -REL_CLIP, REL_CLIP) + REL_CLIP]
    flat = jnp.tile(ext, (1, ATT_TQ))[:, :ATT_TQ * (span - 1)]
    tab = flat.reshape(-1, ATT_TQ, span - 1)[:, :, ATT_TQ - 1:ATT_TQ - 1 + nk]
    r = jnp.arange(ATT_TQ)[:, None]
    m = jnp.arange(nk)[None, :]
    qc = r // CHUNK
    kc = m // CHUNK
    in_band = (kc >= qc) & (kc <= qc + BAND_CHUNKS - 1)
    return jnp.where(in_band[None], tab, NEG_BIG)


def _band_attention(proj, q_blk, k_blk, v_blk, q_norm_w, k_norm_w, rel_bias):
    t = proj.shape[0]
    bias = _attn_bias_table(rel_bias)
    blk = (ATT_TQ, ATT_W)
    return pl.pallas_call(
        _attn_kernel,
        grid=(t // ATT_TQ,),
        in_specs=[pl.BlockSpec(blk, lambda i: (i, q_blk)),
                  pl.BlockSpec(blk, lambda i: (i, k_blk)),
                  pl.BlockSpec(blk, lambda i: (i, v_blk)),
                  pl.BlockSpec(bias.shape, lambda i: (0, 0, 0), pipeline_mode=pl.Buffered(1)),
                  pl.BlockSpec((1, ATT_DH), lambda i: (0, 0)),
                  pl.BlockSpec((1, ATT_DH), lambda i: (0, 0))],
        out_specs=pl.BlockSpec(blk, lambda i: (i, 0)),
        out_shape=jax.ShapeDtypeStruct((t, ATT_W), BF16),
        scratch_shapes=[pltpu.VMEM((ATT_KB, ATT_TQ, ATT_W), BF16), pltpu.VMEM((ATT_KB, ATT_TQ, ATT_W), BF16)],
        compiler_params=_params(("arbitrary",)),
        name="band_attention",
    )(proj, proj, proj, bias, q_norm_w.reshape(1, ATT_DH), k_norm_w.reshape(1, ATT_DH))


GDN_TB = 1024
GDN_NC = GDN_TB // CHUNK
GDN_HPS = 4
GDN_GRP = 4


def _causal_conv(cur_ref, prev_ref, w_ref, scr, first_block):
    rows = cur_ref.shape[0]
    prev = prev_ref[...]
    scr[0:HALO, :] = jnp.where(first_block, jnp.zeros_like(prev), prev)
    scr[HALO:HALO + rows, :] = cur_ref[...]
    w = w_ref[...]
    ext = scr[...]
    acc = ext[HALO:, :] * w[CONV_K - 1:CONV_K, :]
    for s in range(1, CONV_K):
        acc += pltpu.roll(ext, s, 0)[HALO:, :] * w[CONV_K - 1 - s:CONV_K - s, :]
    return acc


def _block_diag(h):
    blk = lax.broadcasted_iota(jnp.int32, (1, h.shape[1]), 1) // CHUNK
    return jnp.concatenate([jnp.where(blk == j, h, jnp.zeros_like(h)) for j in range(h.shape[1] // CHUNK)], axis=0)


def _gdn_kernel(qc_ref, qp_ref, kc_ref, kp_ref, vc_ref, vp_ref, wq_ref, wk_ref, wv_ref,
                gate_ref, alog_ref, dtb_ref, z_ref, nw_ref, o_ref,
                cq_scr, ck_scr, cv_scr, beta_scr, gcum_scr, gt_scr,
                wq_scr, u_scr, qk_scr, a_scr, b_scr, s_scr):
    i = pl.program_id(0)
    hp = pl.program_id(1)
    tb = GDN_TB
    first = i == 0

    @pl.when(first)
    def _():
        for hl in range(GDN_HPS):
            s_scr[hp * GDN_HPS + hl] = jnp.zeros(s_scr.shape[1:], F32)

    @pl.when(hp == 0)
    def _():
        gate = gate_ref[...]
        beta_scr[...] = jax.nn.sigmoid(gate)
        g = -jnp.exp(alog_ref[...]) * _softplus(gate + dtb_ref[...])
        gcum = _chunk_cumsum(g)
        gcum_scr[...] = gcum
        for s in range(tb // LANES):
            gt_scr[s] = gcum[s * LANES:(s + 1) * LANES, :].T

    lane = lax.broadcasted_iota(jnp.int32, (1, LANES), 1)
    r = lax.broadcasted_iota(jnp.int32, (CHUNK, CHUNK), 0)
    c = lax.broadcasted_iota(jnp.int32, (CHUNK, CHUNK), 1)
    causal = r >= c
    strict = r > c
    rh = lax.broadcasted_iota(jnp.int32, (CHUNK, GDN_GRP * CHUNK), 0)
    ch = lax.broadcasted_iota(jnp.int32, (CHUNK, GDN_GRP * CHUNK), 1)
    eye_h = jnp.where(rh == ch % CHUNK, 1.0, 0.0)
    nw = nw_ref[...]
    heads = range(GDN_HPS)
    groups = range(GDN_NC // GDN_GRP)
    gw = GDN_GRP * CHUNK

    def l2n(x):
        return x * lax.rsqrt(jnp.sum(x * x, axis=-1, keepdims=True) + NORM_EPS)

    qa = _silu(_causal_conv(qc_ref, qp_ref, wq_ref, cq_scr, first))
    ka = _silu(_causal_conv(kc_ref, kp_ref, wk_ref, ck_scr, first))
    va = _silu(_causal_conv(vc_ref, vp_ref, wv_ref, cv_scr, first))

    gcs, ks, rhs, low_hs, p_hs, x_hs = {}, {}, {}, {}, {}, {}
    for hl in heads:
        h = hp * GDN_HPS + hl
        cols = slice(hl * LANES, (hl + 1) * LANES)
        beta = jnp.sum(jnp.where(lane == h, beta_scr[...], 0.0), axis=1, keepdims=True)
        gc = jnp.sum(jnp.where(lane == GDN_HEADS + h, gcum_scr[...], 0.0), axis=1, keepdims=True)
        q = l2n(qa[:, cols]) * (GDN_DK ** -0.5)
        k = l2n(ka[:, cols])
        eg = jnp.exp(gc)
        kb = k * beta
        rhs[hl] = jnp.concatenate([va[:, cols] * beta, kb * eg], axis=1).astype(BF16)
        qd = (q * eg).astype(BF16)
        gcs[hl], ks[hl] = gc, k
        for grp in groups:
            lows = []
            for j in range(GDN_GRP):
                n = grp * GDN_GRP + j
                rows = slice(n * CHUNK, (n + 1) * CHUNK)
                gc_n = gc[rows]
                grow = gt_scr[n // 2, pl.ds(GDN_HEADS + h, 1), :]
                grow = grow[:, (n % 2) * CHUNK:(n % 2 + 1) * CHUNK]
                decay = jnp.where(causal, jnp.exp(jnp.where(causal, gc_n - grow, 0.0)), 0.0)
                ab = _dot_nt(jnp.concatenate([kb[rows], q[rows]], axis=0), k[rows])
                lows.append(jnp.where(strict, ab[:CHUNK] * decay, 0.0))
                qk_scr[hl, rows, :] = (ab[CHUNK:] * decay).astype(BF16)
                wq_scr[hl, n, CHUNK:2 * CHUNK, :] = qd[rows]
            low_hs[hl, grp] = jnp.concatenate(lows, axis=1)

    for key, low_h in low_hs.items():
        xb = low_h.astype(BF16)
        p_hs[key] = eye_h - low_h
        x_hs[key] = jnp.dot(xb, _block_diag(xb), preferred_element_type=F32)
    for step in range(5):
        for key in low_hs:
            xb = x_hs[key].astype(BF16)
            x_bd = _block_diag(xb)
            p_hs[key] = p_hs[key] + jnp.dot(p_hs[key].astype(BF16), x_bd, preferred_element_type=F32)
            if step < 4:
                x_hs[key] = jnp.dot(xb, x_bd, preferred_element_type=F32)

    for (hl, grp), p_h in p_hs.items():
        g0 = grp * gw
        uw = jnp.dot(_block_diag(p_h.astype(BF16)), rhs[hl][g0:g0 + gw], preferred_element_type=F32)
        u_scr[hl, g0:g0 + gw, :] = uw[:, :GDN_DV]
        for j in range(GDN_GRP):
            n = grp * GDN_GRP + j
            rows = slice(n * CHUNK, (n + 1) * CHUNK)
            uw_n = uw[j * CHUNK:(j + 1) * CHUNK]
            wq_scr[hl, n, 0:CHUNK, :] = uw_n[:, GDN_DV:].astype(BF16)
            gc_n = gcs[hl][rows]
            kd = ks[hl][rows] * jnp.exp(gc_n[CHUNK - 1:CHUNK, :] - gc_n)
            ba = _dot_tn(kd, uw_n)
            b_scr[hl, n] = ba[:, :GDN_DV]
            a_scr[hl, n] = ba[:, GDN_DV:].astype(BF16)

    states = [s_scr[hp * GDN_HPS + hl] for hl in heads]
    for n in range(GDN_NC):
        rows = slice(n * CHUNK, (n + 1) * CHUNK)
        for hl in heads:
            cols = slice(hl * LANES, (hl + 1) * LANES)
            sb = states[hl].astype(BF16)
            ws = jnp.dot(wq_scr[hl, n], sb, preferred_element_type=F32)
            g_last = jnp.exp(gcs[hl][(n + 1) * CHUNK - 1:(n + 1) * CHUNK, :])
            states[hl] = states[hl] * g_last - jnp.dot(a_scr[hl, n], sb, preferred_element_type=F32) + b_scr[hl, n]
            v_new = u_scr[hl, rows, :] - ws[:CHUNK]
            o = ws[CHUNK:] + jnp.dot(qk_scr[hl, rows, :], v_new.astype(BF16), preferred_element_type=F32)
            on = o * lax.rsqrt(jnp.mean(o * o, axis=-1, keepdims=True) + NORM_EPS) * nw
            o_ref[rows, cols] = (on * _silu(z_ref[rows, cols])).astype(o_ref.dtype)
    for hl in heads:
        s_scr[hp * GDN_HPS + hl] = states[hl]


def _gated_deltanet(proj, gates, conv_w, a_log, dt_bias, norm_w):
    t = proj.shape[0]
    tb = GDN_TB
    nh = GDN_HEADS
    w = GDN_HPS * LANES
    nb = GDN_QK // w
    pad = lambda a: jnp.zeros((1, LANES), F32).at[0, nh:2 * nh].set(a.astype(F32))
    cur = lambda sec: pl.BlockSpec((tb, w), lambda i, h: (i, sec * nb + h))
    prev = lambda sec: pl.BlockSpec((HALO, w), lambda i, h: (jnp.maximum(i * (tb // HALO) - 1, 0), sec * nb + h))
    cw = lambda sec: pl.BlockSpec((CONV_K, w), lambda i, h: (0, sec * nb + h))
    row = pl.BlockSpec((1, LANES), lambda i, h: (0, 0))
    return pl.pallas_call(
        _gdn_kernel,
        grid=(t // tb, nh // GDN_HPS),
        in_specs=[cur(0), prev(0), cur(1), prev(1), cur(2), prev(2), cw(0), cw(1), cw(2),
                  pl.BlockSpec((tb, LANES), lambda i, h: (i, 0)), row, row, cur(3), row],
        out_specs=pl.BlockSpec((tb, w), lambda i, h: (i, h)),
        out_shape=jax.ShapeDtypeStruct((t, GDN_V), BF16),
        scratch_shapes=[pltpu.VMEM((tb + HALO, w), F32)] * 3 + [
            pltpu.VMEM((tb, LANES), F32), pltpu.VMEM((tb, LANES), F32),
            pltpu.VMEM((tb // LANES, LANES, LANES), F32),
            pltpu.VMEM((GDN_HPS, GDN_NC, 2 * CHUNK, GDN_DK), BF16),
            pltpu.VMEM((GDN_HPS, tb, GDN_DV), F32),
            pltpu.VMEM((GDN_HPS, tb, CHUNK), BF16),
            pltpu.VMEM((GDN_HPS, GDN_NC, GDN_DK, GDN_DK), BF16),
            pltpu.VMEM((GDN_HPS, GDN_NC, GDN_DK, GDN_DV), F32),
            pltpu.VMEM((nh, GDN_DK, GDN_DV), F32)],
        compiler_params=_params(("arbitrary", "arbitrary")),
        name="gated_deltanet",
    )(proj, proj, proj, proj, proj, proj, conv_w, conv_w, conv_w,
      gates, pad(a_log), pad(dt_bias), proj, norm_w.reshape(1, GDN_DV))


SSD_TB = 512
SSD_NC = SSD_TB // CHUNK


def _ssd_expand_matrix():
    j = jnp.arange(LANES)[:, None]
    col = jnp.arange(2 * SSD_GW)[None, :]
    head = (col % SSD_GW) // SSD_HEADDIM
    piece = j // SSD_HPG
    is_ac = (col < SSD_GW) & (piece < 3)
    is_dt = (col >= SSD_GW) & (piece >= 3) & (piece < 5)
    return jnp.where((is_ac | is_dt) & (j % SSD_HPG == head), 1.0, 0.0).astype(BF16)


def _ssd_kernel(xc_ref, xp_ref, bc_ref, bp_ref, cc_ref, cp_ref, wx_ref, wb_ref, wc_ref,
                bx_ref, bb_ref, bcn_ref, dt_ref, dtb_ref, alog_ref, dsk_ref, z_ref, nw_ref, ex_ref, o_ref,
                cx_scr, cb_scr, cc_scr, dt_scr, ac_scr, x_scr, e_scr, st_scr):
    i = pl.program_id(0)
    g = pl.program_id(1)
    tb = SSD_TB
    hpg = SSD_HPG
    first = i == 0

    @pl.when(first)
    def _():
        st_scr[g] = jnp.zeros(st_scr.shape[1:], F32)

    @pl.when(g == 0)
    def _():
        dt = _softplus(dt_ref[...] + dtb_ref[...])
        dt_scr[...] = dt
        ac_scr[...] = _chunk_cumsum(dt * (-jnp.exp(alog_ref[...])))

    shift = (LANES - g * hpg) % LANES
    lane = lax.broadcasted_iota(jnp.int32, (1, LANES), 1)
    mine = lane < hpg
    dt_g = jnp.where(mine, pltpu.roll(dt_scr[...], shift, 1), 0.0)
    ac_g = jnp.where(mine, pltpu.roll(ac_scr[...], shift, 1), 0.0)

    a1, a2, a3 = _split3(ac_g)
    d1 = dt_g.astype(BF16)
    d2 = (dt_g - d1.astype(F32)).astype(BF16)
    packed = a1.astype(F32)
    for k, piece in enumerate((a2, a3, d1, d2), start=1):
        packed = packed + pltpu.roll(piece.astype(F32), k * hpg, 1)
    e_scr[...] = jnp.dot(packed.astype(BF16), ex_ref[...], preferred_element_type=F32)

    ac_t = [ac_g[s * LANES:(s + 1) * LANES, :].T[0:8, :] for s in range(tb // LANES)]
    ac_tr = [pltpu.roll(a, CHUNK, 1) for a in ac_t]

    x_scr[...] = _silu(_causal_conv(xc_ref, xp_ref, wx_ref, cx_scr, first) + bx_ref[...])
    bm = _silu(_causal_conv(bc_ref, bp_ref, wb_ref, cb_scr, first) + bb_ref[...])
    cm = _silu(_causal_conv(cc_ref, cp_ref, wc_ref, cc_scr, first) + bcn_ref[...])

    r = lax.broadcasted_iota(jnp.int32, (CHUNK, LANES), 0)
    c = lax.broadcasted_iota(jnp.int32, (CHUNK, LANES), 1)
    causal2 = r >= (c % CHUNK)
    left = c < CHUNK
    dsk = dsk_ref[...]
    nw = nw_ref[...]
    gw = SSD_GW

    chunks = range(SSD_NC)
    rows = [slice(n * CHUNK, (n + 1) * CHUNK) for n in chunks]
    cbs = [_dot_nt(cm[rows[n]], bm[rows[n]]) for n in chunks]
    y_diag, upd, dec = [], [], []
    for n in chunks:
        acx = e_scr[rows[n], 0:gw]
        xdt = x_scr[rows[n], :] * e_scr[rows[n], gw:2 * gw]
        a_last = acx[CHUNK - 1:CHUNK, :]
        cb2 = jnp.concatenate([cbs[n], cbs[n]], axis=1)
        t_lo, t_hi = (ac_t[n // 2], ac_tr[n // 2]) if n % 2 == 0 else (ac_tr[n // 2], ac_t[n // 2])
        ys = []
        for p in range(hpg // 2):
            lanes = slice(p * LANES, (p + 1) * LANES)
            a_row = jnp.where(lane < CHUNK, t_lo[2 * p:2 * p + 1, :], t_hi[2 * p + 1:2 * p + 2, :])
            decay = jnp.where(causal2, jnp.exp(jnp.where(causal2, acx[:, lanes] - a_row, 0.0)), 0.0)
            xp = xdt[:, lanes]
            x_bd = jnp.concatenate([jnp.where(left, xp, 0.0), jnp.where(left, 0.0, xp)], axis=0)
            ys.append(_dot(cb2 * decay, x_bd))
        y_diag.append(jnp.concatenate(ys, axis=1))
        upd.append(_dot_tn(bm[rows[n]], xdt * jnp.exp(a_last - acx)))
        dec.append(jnp.exp(a_last))

    state = st_scr[g]
    states = []
    for n in chunks:
        states.append(state.astype(BF16))
        state = state * dec[n] + upd[n]
    st_scr[g] = state

    for n in chunks:
        acx = e_scr[rows[n], 0:gw]
        y_off = jnp.dot(cm[rows[n]].astype(BF16), states[n], preferred_element_type=F32)
        y = y_diag[n] + y_off * jnp.exp(acx) + x_scr[rows[n], :] * dsk
        y = y * _silu(z_ref[rows[n], :])
        y = y * lax.rsqrt(jnp.mean(y * y, axis=-1, keepdims=True) + NORM_EPS) * nw
        o_ref[rows[n], :] = y.astype(o_ref.dtype)


def _mamba2_ssd(proj, dt_raw, conv_w, conv_b, dt_bias, a_log, d_skip, norm_w):
    t = proj.shape[0]
    tb = SSD_TB
    gw = SSD_GW
    ng = SSD_GROUPS
    x0 = SSD_DINNER // gw
    b0 = 2 * SSD_DINNER // LANES
    c0 = b0 + ng
    pad = lambda a: jnp.zeros((1, LANES), F32).at[0, :SSD_HEADS].set(a.astype(F32))
    prev_idx = lambda i: jnp.maximum(i * (tb // HALO) - 1, 0)
    cbias = conv_b.reshape(1, -1)
    dsk = jnp.repeat(d_skip.astype(F32), SSD_HEADDIM).reshape(1, SSD_DINNER)
    return pl.pallas_call(
        _ssd_kernel,
        grid=(t // tb, ng),
        in_specs=[pl.BlockSpec((tb, gw), lambda i, g: (i, x0 + g)),
                  pl.BlockSpec((HALO, gw), lambda i, g: (prev_idx(i), x0 + g)),
                  pl.BlockSpec((tb, LANES), lambda i, g: (i, b0 + g)),
                  pl.BlockSpec((HALO, LANES), lambda i, g: (prev_idx(i), b0 + g)),
                  pl.BlockSpec((tb, LANES), lambda i, g: (i, c0 + g)),
                  pl.BlockSpec((HALO, LANES), lambda i, g: (prev_idx(i), c0 + g)),
                  pl.BlockSpec((CONV_K, gw), lambda i, g: (0, g)),
                  pl.BlockSpec((CONV_K, LANES), lambda i, g: (0, SSD_DINNER // LANES + g)),
                  pl.BlockSpec((CONV_K, LANES), lambda i, g: (0, SSD_DINNER // LANES + ng + g)),
                  pl.BlockSpec((1, gw), lambda i, g: (0, g)),
                  pl.BlockSpec((1, LANES), lambda i, g: (0, SSD_DINNER // LANES + g)),
                  pl.BlockSpec((1, LANES), lambda i, g: (0, SSD_DINNER // LANES + ng + g)),
                  pl.BlockSpec((tb, LANES), lambda i, g: (i, 0)),
                  pl.BlockSpec((1, LANES), lambda i, g: (0, 0)),
                  pl.BlockSpec((1, LANES), lambda i, g: (0, 0)),
                  pl.BlockSpec((1, gw), lambda i, g: (0, g)),
                  pl.BlockSpec((tb, gw), lambda i, g: (i, g)),
                  pl.BlockSpec((1, gw), lambda i, g: (0, g)),
                  pl.BlockSpec((LANES, 2 * gw), lambda i, g: (0, 0))],
        out_specs=pl.BlockSpec((tb, gw), lambda i, g: (i, g)),
        out_shape=jax.ShapeDtypeStruct((t, SSD_DINNER), BF16),
        scratch_shapes=[pltpu.VMEM((tb + HALO, gw), F32), pltpu.VMEM((tb + HALO, LANES), F32),
                        pltpu.VMEM((tb + HALO, LANES), F32),
                        pltpu.VMEM((tb, LANES), F32), pltpu.VMEM((tb, LANES), F32),
                        pltpu.VMEM((tb, gw), F32), pltpu.VMEM((tb, 2 * gw), F32),
                        pltpu.VMEM((ng, SSD_DSTATE, gw), F32)],
        compiler_params=_params(("arbitrary", "arbitrary")),
        name="mamba2_ssd",
    )(proj, proj, proj, proj, proj, proj, conv_w, conv_w, conv_w, cbias, cbias, cbias,
      dt_raw, pad(dt_bias), pad(a_log), dsk, proj, norm_w.reshape(1, SSD_DINNER), _ssd_expand_matrix())


def _layer_mods(mod, layer):
    d = D_MODEL
    return [mod[layer, k * d:(k + 1) * d].reshape(1, d) for k in range(6)]


def kernel(x, c, mod_w, mod_b, norm_mix_w, norm_mlp_w, mlp_w1, mlp_w2, ab_w_in, gdn_conv_w, gdn_a_log,
           gdn_dt_bias, gdn_norm_w, attn_q_norm_w, attn_k_norm_w, attn_rel_bias, ab_w_out, ssd_w_in,
           ssd_conv_w, ssd_conv_b, ssd_dt_bias, ssd_a_log, ssd_d, ssd_norm_w, ssd_w_out):
    b, t, d = x.shape
    assert b == 1 and d == D_MODEL
    xs = x.reshape(t, d)
    mod = _modulation(c, mod_w, mod_b)

    sh1, sc1, g1, sh2, sc2, g2 = _layer_mods(mod, 0)
    w_in = ab_w_in[0]
    gate_lo = 2 * GDN_QK + 2 * GDN_V
    gate_hi = gate_lo + 2 * GDN_HEADS
    w_in_b = w_in.astype(BF16)
    w_main = jnp.concatenate([w_in_b[:, :gate_lo], w_in_b[:, gate_hi:]], axis=1)
    w_gate = jnp.pad(w_in[:, gate_lo:gate_hi], ((0, 0), (0, LANES - 2 * GDN_HEADS)))
    proj, gates = _norm_proj(xs, norm_mix_w[0].reshape(1, d), sc1, sh1, w_main, w_gate, tm=512)
    o_a = _gated_deltanet(proj, gates, gdn_conv_w[0], gdn_a_log[0], gdn_dt_bias[0], gdn_norm_w[0])
    a0 = gate_lo // ATT_W
    o_b = _band_attention(proj, a0, a0 + 1, a0 + 2, attn_q_norm_w[0], attn_k_norm_w[0], attn_rel_bias[0])
    xs, h2 = _out_proj([o_a, o_b], [ab_w_out[0][:GDN_V], ab_w_out[0][GDN_V:]], xs, g1,
                       norm_mlp_w[0].reshape(1, d), sc2, sh2)
    xs = _mlp(xs, h2, g2, mlp_w1[0], mlp_w2[0])

    sh1, sc1, g1, sh2, sc2, g2 = _layer_mods(mod, 1)
    w_in = ssd_w_in[0]
    n_main = 2 * SSD_DINNER + 2 * SSD_GROUPS * SSD_DSTATE
    w_dt = jnp.pad(w_in[:, n_main:], ((0, 0), (0, LANES - SSD_HEADS)))
    proj, dt_raw = _norm_proj(xs, norm_mix_w[1].reshape(1, d), sc1, sh1, w_in[:, :n_main], w_dt, tm=256)
    y = _mamba2_ssd(proj, dt_raw, ssd_conv_w[0], ssd_conv_b[0], ssd_dt_bias[0], ssd_a_log[0],
                    ssd_d[0], ssd_norm_w[0])
    xs, h2 = _out_proj([y], [ssd_w_out[0]], xs, g1, norm_mlp_w[1].reshape(1, d), sc2, sh2)
    xs = _mlp(xs, h2, g2, mlp_w1[1], mlp_w2[1])
    return xs.reshape(b, t, d)
```

```python
import functools

import jax
import jax.numpy as jnp
from jax import lax
from jax.experimental import pallas as pl
from jax.experimental.pallas import tpu as pltpu

F32 = jnp.float32
BF16 = jnp.bfloat16

D_MODEL = 2048
CHUNK = 64
NORM_EPS = 1e-6
CONV_K = 4
HALO = 8
GDN_HEADS = 8
GDN_DK = 128
GDN_DV = 128
GDN_QK = GDN_HEADS * GDN_DK
GDN_V = GDN_HEADS * GDN_DV
ATT_HEADS = 8
ATT_DH = 128
ATT_W = ATT_HEADS * ATT_DH
BAND_CHUNKS = 9
REL_CLIP = 256
SSD_DINNER = 2 * D_MODEL
SSD_HEADDIM = 64
SSD_HEADS = SSD_DINNER // SSD_HEADDIM
SSD_GROUPS = 8
SSD_HPG = SSD_HEADS // SSD_GROUPS
SSD_DSTATE = 128
SSD_GW = SSD_HPG * SSD_HEADDIM
D_FF = 4 * D_MODEL
LANES = 128
NEG_BIG = -1e30

VMEM_LIMIT = 56 * 1024 * 1024

_NT = (((1,), (1,)), ((), ()))
_TN = (((0,), (0,)), ((), ()))


def _dot(a, b):
    return jnp.dot(a.astype(BF16), b.astype(BF16), preferred_element_type=F32)


def _dot_nt(a, b):
    return lax.dot_general(a.astype(BF16), b.astype(BF16), _NT, preferred_element_type=F32)


def _dot_tn(a, b):
    return lax.dot_general(a.astype(BF16), b.astype(BF16), _TN, preferred_element_type=F32)


def _split3(a):
    a1 = a.astype(BF16)
    r1 = a - a1.astype(F32)
    a2 = r1.astype(BF16)
    a3 = (r1 - a2.astype(F32)).astype(BF16)
    return a1, a2, a3


def _dot_exact_lhs(a_bf16, b):
    b1, b2, b3 = _split3(b)
    out = jnp.dot(a_bf16, b1, preferred_element_type=F32)
    out += jnp.dot(a_bf16, b2, preferred_element_type=F32)
    out += jnp.dot(a_bf16, b3, preferred_element_type=F32)
    return out


def _silu(x):
    return x * jax.nn.sigmoid(x)


def _softplus(x):
    return jnp.maximum(x, 0.0) + jnp.log(1.0 + jnp.exp(-jnp.abs(x)))


def _norm_mod(x, nw, sc, sh):
    ms = jnp.mean(x * x, axis=-1, keepdims=True)
    return (x * lax.rsqrt(ms + NORM_EPS) * nw) * (1.0 + sc) + sh


def _seg_tri(n):
    r = lax.broadcasted_iota(jnp.int32, (n, n), 0)
    c = lax.broadcasted_iota(jnp.int32, (n, n), 1)
    return jnp.where((r >= c) & ((r // CHUNK) == (c // CHUNK)), 1.0, 0.0).astype(BF16)


MXU_K = 256


def _chunk_cumsum(g):
    tri = _seg_tri(MXU_K)
    return jnp.concatenate([_dot_exact_lhs(tri, g[s:s + MXU_K]) for s in range(0, g.shape[0], MXU_K)], axis=0)


def _params(sem):
    return pltpu.CompilerParams(dimension_semantics=sem, vmem_limit_bytes=VMEM_LIMIT)


def _mod_kernel(c_ref, w_ref, b_ref, o_ref):
    c = c_ref[...]
    o_ref[...] = jnp.dot(_silu(c), w_ref[...], preferred_element_type=F32,
                         precision=lax.Precision.HIGHEST) + b_ref[...]


def _modulation(c, mod_w, mod_b, tn=1024):
    depth, d, n = mod_w.shape
    c8 = jnp.broadcast_to(c.reshape(1, d), (8, d))
    out = pl.pallas_call(
        _mod_kernel,
        grid=(depth, n // tn),
        in_specs=[pl.BlockSpec((8, d), lambda l, j: (0, 0)),
                  pl.BlockSpec((None, d, tn), lambda l, j: (l, 0, j)),
                  pl.BlockSpec((None, 1, tn), lambda l, j: (l, 0, j))],
        out_specs=pl.BlockSpec((None, 8, tn), lambda l, j: (l, 0, j)),
        out_shape=jax.ShapeDtypeStruct((depth, 8, n), F32),
        compiler_params=_params(("arbitrary", "arbitrary")),
        name="modulation",
    )(c8, mod_w, mod_b.reshape(depth, 1, n))
    return out[:, 0, :]


def _proj_kernel(x_ref, nw_ref, sc_ref, sh_ref, w_ref, wg_ref, o_ref, g_ref):
    h = _norm_mod(x_ref[...], nw_ref[...], sc_ref[...], sh_ref[...])
    hb = h.astype(BF16)
    o_ref[...] = jnp.dot(hb, w_ref[...], preferred_element_type=F32)
    ng = g_ref.shape[-1]

    @pl.when(pl.program_id(0) == 0)
    def _():
        r = jnp.dot(hb, wg_ref[...], preferred_element_type=F32)
        g_ref[...] = r[:, :ng] + r[:, ng:]

    @pl.when(pl.program_id(0) != 0)
    def _():
        g_ref[...] = jnp.zeros_like(g_ref)


def _split_gate_weights(wg):
    wg1 = wg.astype(BF16)
    wg2 = (wg - wg1.astype(F32)).astype(BF16)
    return jnp.concatenate([wg1, wg2], axis=1)


def _norm_proj(x, nw, sc, sh, w_bf16, n, wg, tm, n_split=2):
    t, d = x.shape
    ng = wg.shape[1]
    tn = n // n_split
    row = lambda s, i: (0, 0)
    out, gates = pl.pallas_call(
        _proj_kernel,
        grid=(n_split, t // tm),
        in_specs=[pl.BlockSpec((tm, d), lambda s, i: (i, 0)),
                  pl.BlockSpec((1, d), row), pl.BlockSpec((1, d), row), pl.BlockSpec((1, d), row),
                  pl.BlockSpec((d, tn), lambda s, i: (0, s), pipeline_mode=pl.Buffered(1)),
                  pl.BlockSpec((d, 2 * ng), row, pipeline_mode=pl.Buffered(1))],
        out_specs=[pl.BlockSpec((tm, tn), lambda s, i: (i, s)),
                   pl.BlockSpec((None, tm, ng), lambda s, i: (s, i, 0))],
        out_shape=[jax.ShapeDtypeStruct((t, n), F32), jax.ShapeDtypeStruct((n_split, t, ng), F32)],
        compiler_params=_params(("arbitrary", "arbitrary")),
        name="norm_proj",
    )(x, nw, sc, sh, w_bf16, _split_gate_weights(wg))
    return out, gates[0]


def _proj2_kernel(x_ref, nw_ref, sc_ref, sh_ref, wa_ref, wb_ref, wg_ref, o_ref, g_ref):
    h = _norm_mod(x_ref[...], nw_ref[...], sc_ref[...], sh_ref[...])
    hb = h.astype(BF16)
    na = wa_ref.shape[1]
    o_ref[:, :na] = jnp.dot(hb, wa_ref[...], preferred_element_type=F32)
    o_ref[:, na:] = jnp.dot(hb, wb_ref[...], preferred_element_type=F32)
    ng = g_ref.shape[-1]
    r = jnp.dot(hb, wg_ref[...], preferred_element_type=F32)
    g_ref[...] = r[:, :ng] + r[:, ng:]


def _norm_proj2(x, nw, sc, sh, wa_bf16, na, wb_bf16, wg, tm):
    t, d = x.shape
    nb = wb_bf16.shape[1]
    ng = wg.shape[1]
    row = lambda i: (0, 0)
    return pl.pallas_call(
        _proj2_kernel,
        grid=(t // tm,),
        in_specs=[pl.BlockSpec((tm, d), lambda i: (i, 0)),
                  pl.BlockSpec((1, d), row), pl.BlockSpec((1, d), row), pl.BlockSpec((1, d), row),
                  pl.BlockSpec((d, na), row, pipeline_mode=pl.Buffered(1)),
                  pl.BlockSpec((d, nb), row, pipeline_mode=pl.Buffered(1)),
                  pl.BlockSpec((d, 2 * ng), row, pipeline_mode=pl.Buffered(1))],
        out_specs=[pl.BlockSpec((tm, na + nb), lambda i: (i, 0)), pl.BlockSpec((tm, ng), lambda i: (i, 0))],
        out_shape=[jax.ShapeDtypeStruct((t, na + nb), F32), jax.ShapeDtypeStruct((t, ng), F32)],
        compiler_params=_params(("arbitrary",)),
        name="norm_proj2",
    )(x, nw, sc, sh, wa_bf16, wb_bf16, _split_gate_weights(wg))


def _out_kernel(n_in, *refs):
    y_refs = refs[:n_in]
    w_refs = refs[n_in:2 * n_in]
    x_ref, g_ref, nw_ref, sc_ref, sh_ref, o_ref, h_ref = refs[2 * n_in:]
    acc = jnp.dot(y_refs[0][...], w_refs[0][...], preferred_element_type=F32)
    for y_ref, w_ref in zip(y_refs[1:], w_refs[1:]):
        acc += jnp.dot(y_ref[...], w_ref[...], preferred_element_type=F32)
    x_new = x_ref[...] + g_ref[...] * acc
    o_ref[...] = x_new
    h_ref[...] = _norm_mod(x_new, nw_ref[...], sc_ref[...], sh_ref[...]).astype(h_ref.dtype)


def _out_proj(ys, ws, x, gate, nw, sc, sh, tm=512):
    t, d = x.shape
    n_in = len(ys)
    row = pl.BlockSpec((1, d), lambda i: (0, 0))
    in_specs = [pl.BlockSpec((tm, y.shape[1]), lambda i: (i, 0)) for y in ys]
    in_specs += [pl.BlockSpec(w.shape, lambda i: (0, 0), pipeline_mode=pl.Buffered(1)) for w in ws]
    in_specs += [pl.BlockSpec((tm, d), lambda i: (i, 0)), row, row, row, row]
    return pl.pallas_call(
        functools.partial(_out_kernel, n_in),
        grid=(t // tm,),
        in_specs=in_specs,
        out_specs=[pl.BlockSpec((tm, d), lambda i: (i, 0)), pl.BlockSpec((tm, d), lambda i: (i, 0))],
        out_shape=[jax.ShapeDtypeStruct((t, d), F32), jax.ShapeDtypeStruct((t, d), BF16)],
        compiler_params=_params(("arbitrary",)),
        name="out_proj",
    )(*ys, *[w.astype(BF16) for w in ws], x, gate, nw, sc, sh)


def _mlp_kernel(x_ref, h_ref, g_ref, w1_ref, w2_ref, o_ref):
    f = pl.program_id(1)

    @pl.when(f == 0)
    def _():
        o_ref[...] = jnp.zeros_like(o_ref)

    a = jnp.maximum(jnp.dot(h_ref[...], w1_ref[...], preferred_element_type=F32), 0.0)
    o_ref[...] += jnp.dot((a * a).astype(BF16), w2_ref[...], preferred_element_type=F32)

    @pl.when(f == pl.num_programs(1) - 1)
    def _():
        o_ref[...] = x_ref[...] + g_ref[...] * o_ref[...]


def _mlp(x, h, gate, w1_all, w2_all, layer, tm=1024, tf=512):
    t, d = x.shape
    ff = w1_all.shape[2]
    return pl.pallas_call(
        _mlp_kernel,
        grid=(t // tm, ff // tf),
        in_specs=[pl.BlockSpec((tm, d), lambda i, f: (i, 0)),
                  pl.BlockSpec((tm, d), lambda i, f: (i, 0)),
                  pl.BlockSpec((1, d), lambda i, f: (0, 0)),
                  pl.BlockSpec((None, d, tf), lambda i, f: (layer, 0, f)),
                  pl.BlockSpec((None, tf, d), lambda i, f: (layer, f, 0))],
        out_specs=pl.BlockSpec((tm, d), lambda i, f: (i, 0)),
        out_shape=jax.ShapeDtypeStruct((t, d), F32),
        compiler_params=_params(("parallel", "arbitrary")),
        name="mlp",
    )(x, h, gate, w1_all, w2_all)


ATT_TQ = 256
ATT_KB = 3
ATT_HG = 4


def _attn_kernel(q_ref, k_ref, v_ref, bias_ref, qw_ref, kw_ref, o_ref, kn_scr, v_scr):
    i = pl.program_id(0)

    @pl.when(i == 0)
    def _():
        kn_scr[...] = jnp.zeros_like(kn_scr)
        v_scr[...] = jnp.zeros_like(v_scr)

    def rms(x, w):
        return x * lax.rsqrt(jnp.mean(x * x, axis=-1, keepdims=True) + NORM_EPS) * w

    slots = [lax.rem(i + 1 + b, ATT_KB) for b in range(ATT_KB)]
    kw = kw_ref[...]
    qw = qw_ref[...]
    for h in range(ATT_HEADS):
        cols = slice(h * ATT_DH, (h + 1) * ATT_DH)
        kn_scr[slots[-1], :, cols] = rms(k_ref[:, cols], kw).astype(BF16)
    v_scr[slots[-1]] = v_ref[...].astype(BF16)

    for h0 in range(0, ATT_HEADS, ATT_HG):
        hs = range(h0, h0 + ATT_HG)
        col = {h: slice(h * ATT_DH, (h + 1) * ATT_DH) for h in hs}
        q = {h: (rms(q_ref[:, col[h]], qw) * (ATT_DH ** -0.5)).astype(BF16) for h in hs}
        s = {}
        for h in hs:
            for b in range(ATT_KB):
                sb = lax.dot_general(q[h], kn_scr[slots[b], :, col[h]], _NT, preferred_element_type=F32)
                sb = sb + bias_ref[h, :, b * ATT_TQ:(b + 1) * ATT_TQ]
                s[h, b] = jnp.where(i >= ATT_KB - 1 - b, sb, NEG_BIG)
        p, l = {}, {}
        for h in hs:
            m = jnp.max(jnp.maximum(jnp.maximum(s[h, 0], s[h, 1]), s[h, 2]), axis=-1, keepdims=True)
            for b in range(ATT_KB):
                p[h, b] = jnp.exp(s[h, b] - m)
            l[h] = jnp.sum(p[h, 0] + p[h, 1] + p[h, 2], axis=-1, keepdims=True)
        for h in hs:
            o = jnp.dot(p[h, 0].astype(BF16), v_scr[slots[0], :, col[h]], preferred_element_type=F32)
            for b in range(1, ATT_KB):
                o += jnp.dot(p[h, b].astype(BF16), v_scr[slots[b], :, col[h]], preferred_element_type=F32)
            o_ref[:, col[h]] = (o / l[h]).astype(o_ref.dtype)


def _attn_bias_table(rel_bias):
    nk = ATT_KB * ATT_TQ
    span = ATT_TQ + nk
    dist = jnp.arange(span) - (ATT_TQ - 1) - (ATT_KB - 1) * ATT_TQ
    ext = rel_bias.astype(F32)[:, jnp.clip(dist, -REL_CLIP, REL_CLIP) + REL_CLIP]
    flat = jnp.tile(ext, (1, ATT_TQ))[:, :ATT_TQ * (span - 1)]
    tab = flat.reshape(-1, ATT_TQ, span - 1)[:, :, ATT_TQ - 1:ATT_TQ - 1 + nk]
    r = jnp.arange(ATT_TQ)[:, None]
    m = jnp.arange(nk)[None, :]
    qc = r // CHUNK
    kc = m // CHUNK
    in_band = (kc >= qc) & (kc <= qc + BAND_CHUNKS - 1)
    return jnp.where(in_band[None], tab, NEG_BIG)


def _band_attention(proj, q_blk, k_blk, v_blk, q_norm_w, k_norm_w, rel_bias):
    t = proj.shape[0]
    bias = _attn_bias_table(rel_bias)
    blk = (ATT_TQ, ATT_W)
    return pl.pallas_call(
        _attn_kernel,
        grid=(t // ATT_TQ,),
        in_specs=[pl.BlockSpec(blk, lambda i: (i, q_blk)),
                  pl.BlockSpec(blk, lambda i: (i, k_blk)),
                  pl.BlockSpec(blk, lambda i: (i, v_blk)),
                  pl.BlockSpec(bias.shape, lambda i: (0, 0, 0), pipeline_mode=pl.Buffered(1)),
                  pl.BlockSpec((1, ATT_DH), lambda i: (0, 0)),
                  pl.BlockSpec((1, ATT_DH), lambda i: (0, 0))],
        out_specs=pl.BlockSpec(blk, lambda i: (i, 0)),
        out_shape=jax.ShapeDtypeStruct((t, ATT_W), BF16),
        scratch_shapes=[pltpu.VMEM((ATT_KB, ATT_TQ, ATT_W), BF16), pltpu.VMEM((ATT_KB, ATT_TQ, ATT_W), BF16)],
        compiler_params=_params(("arbitrary",)),
        name="band_attention",
    )(proj, proj, proj, bias, q_norm_w.reshape(1, ATT_DH), k_norm_w.reshape(1, ATT_DH))


GDN_TB = 1024
GDN_NC = GDN_TB // CHUNK
GDN_HPS = 4
GDN_GRP = 4


def _causal_conv(cur_ref, prev_ref, w_ref, scr, first_block):
    rows = cur_ref.shape[0]
    prev = prev_ref[...]
    scr[0:HALO, :] = jnp.where(first_block, jnp.zeros_like(prev), prev)
    scr[HALO:HALO + rows, :] = cur_ref[...]
    w = w_ref[...]
    ext = scr[...]
    acc = ext[HALO:, :] * w[CONV_K - 1:CONV_K, :]
    for s in range(1, CONV_K):
        acc += pltpu.roll(ext, s, 0)[HALO:, :] * w[CONV_K - 1 - s:CONV_K - s, :]
    return acc


def _block_diag(h):
    blk = lax.broadcasted_iota(jnp.int32, (1, h.shape[1]), 1) // CHUNK
    return jnp.concatenate([jnp.where(blk == j, h, jnp.zeros_like(h)) for j in range(h.shape[1] // CHUNK)], axis=0)


def _gdn_kernel(qc_ref, qp_ref, kc_ref, kp_ref, vc_ref, vp_ref, wq_ref, wk_ref, wv_ref,
                gate_ref, alog_ref, dtb_ref, z_ref, nw_ref, o_ref,
                cq_scr, ck_scr, cv_scr, beta_scr, gcum_scr, gt_scr,
                wq_scr, u_scr, qk_scr, a_scr, b_scr, s_scr):
    i = pl.program_id(0)
    hp = pl.program_id(1)
    tb = GDN_TB
    first = i == 0

    @pl.when(first)
    def _():
        for hl in range(GDN_HPS):
            s_scr[hp * GDN_HPS + hl] = jnp.zeros(s_scr.shape[1:], F32)

    @pl.when(hp == 0)
    def _():
        gate = gate_ref[...]
        beta_scr[...] = jax.nn.sigmoid(gate)
        g = -jnp.exp(alog_ref[...]) * _softplus(gate + dtb_ref[...])
        gcum = _chunk_cumsum(g)
        gcum_scr[...] = gcum
        for s in range(tb // LANES):
            gt_scr[s] = gcum[s * LANES:(s + 1) * LANES, :].T

    lane = lax.broadcasted_iota(jnp.int32, (1, LANES), 1)
    r = lax.broadcasted_iota(jnp.int32, (CHUNK, CHUNK), 0)
    c = lax.broadcasted_iota(jnp.int32, (CHUNK, CHUNK), 1)
    causal = r >= c
    strict = r > c
    rh = lax.broadcasted_iota(jnp.int32, (CHUNK, GDN_GRP * CHUNK), 0)
    ch = lax.broadcasted_iota(jnp.int32, (CHUNK, GDN_GRP * CHUNK), 1)
    eye_h = jnp.where(rh == ch % CHUNK, 1.0, 0.0)
    nw = nw_ref[...]
    heads = range(GDN_HPS)
    groups = range(GDN_NC // GDN_GRP)
    gw = GDN_GRP * CHUNK

    def l2n(x):
        return x * lax.rsqrt(jnp.sum(x * x, axis=-1, keepdims=True) + NORM_EPS)

    qa = _silu(_causal_conv(qc_ref, qp_ref, wq_ref, cq_scr, first))
    ka = _silu(_causal_conv(kc_ref, kp_ref, wk_ref, ck_scr, first))
    va = _silu(_causal_conv(vc_ref, vp_ref, wv_ref, cv_scr, first))

    gcs, ks, rhs, low_hs, p_hs, x_hs = {}, {}, {}, {}, {}, {}
    for hl in heads:
        h = hp * GDN_HPS + hl
        cols = slice(hl * LANES, (hl + 1) * LANES)
        beta = jnp.sum(jnp.where(lane == h, beta_scr[...], 0.0), axis=1, keepdims=True)
        gc = jnp.sum(jnp.where(lane == GDN_HEADS + h, gcum_scr[...], 0.0), axis=1, keepdims=True)
        q = l2n(qa[:, cols]) * (GDN_DK ** -0.5)
        k = l2n(ka[:, cols])
        eg = jnp.exp(gc)
        kb = k * beta
        rhs[hl] = jnp.concatenate([va[:, cols] * beta, kb * eg], axis=1).astype(BF16)
        qd = (q * eg).astype(BF16)
        gcs[hl], ks[hl] = gc, k
        for grp in groups:
            lows = []
            for j in range(GDN_GRP):
                n = grp * GDN_GRP + j
                rows = slice(n * CHUNK, (n + 1) * CHUNK)
                gc_n = gc[rows]
                grow = gt_scr[n // 2, pl.ds(GDN_HEADS + h, 1), :]
                grow = grow[:, (n % 2) * CHUNK:(n % 2 + 1) * CHUNK]
                decay = jnp.where(causal, jnp.exp(jnp.where(causal, gc_n - grow, 0.0)), 0.0)
                ab = _dot_nt(jnp.concatenate([kb[rows], q[rows]], axis=0), k[rows])
                lows.append(jnp.where(strict, ab[:CHUNK] * decay, 0.0))
                qk_scr[hl, rows, :] = (ab[CHUNK:] * decay).astype(BF16)
                wq_scr[hl, n, CHUNK:2 * CHUNK, :] = qd[rows]
            low_hs[hl, grp] = jnp.concatenate(lows, axis=1)

    for key, low_h in low_hs.items():
        xb = low_h.astype(BF16)
        p_hs[key] = eye_h - low_h
        x_hs[key] = jnp.dot(xb, _block_diag(xb), preferred_element_type=F32)
    for step in range(5):
        for key in low_hs:
            xb = x_hs[key].astype(BF16)
            x_bd = _block_diag(xb)
            p_hs[key] = p_hs[key] + jnp.dot(p_hs[key].astype(BF16), x_bd, preferred_element_type=F32)
            if step < 4:
                x_hs[key] = jnp.dot(xb, x_bd, preferred_element_type=F32)

    for (hl, grp), p_h in p_hs.items():
        g0 = grp * gw
        uw = jnp.dot(_block_diag(p_h.astype(BF16)), rhs[hl][g0:g0 + gw], preferred_element_type=F32)
        u_scr[hl, g0:g0 + gw, :] = uw[:, :GDN_DV]
        for j in range(GDN_GRP):
            n = grp * GDN_GRP + j
            rows = slice(n * CHUNK, (n + 1) * CHUNK)
            uw_n = uw[j * CHUNK:(j + 1) * CHUNK]
            wq_scr[hl, n, 0:CHUNK, :] = uw_n[:, GDN_DV:].astype(BF16)
            gc_n = gcs[hl][rows]
            kd = ks[hl][rows] * jnp.exp(gc_n[CHUNK - 1:CHUNK, :] - gc_n)
            ba = _dot_tn(kd, uw_n)
            b_scr[hl, n] = ba[:, :GDN_DV]
            a_scr[hl, n] = ba[:, GDN_DV:].astype(BF16)

    states = [s_scr[hp * GDN_HPS + hl] for hl in heads]
    for n in range(GDN_NC):
        rows = slice(n * CHUNK, (n + 1) * CHUNK)
        for hl in heads:
            cols = slice(hl * LANES, (hl + 1) * LANES)
            sb = states[hl].astype(BF16)
            ws = jnp.dot(wq_scr[hl, n], sb, preferred_element_type=F32)
            g_last = jnp.exp(gcs[hl][(n + 1) * CHUNK - 1:(n + 1) * CHUNK, :])
            states[hl] = states[hl] * g_last - jnp.dot(a_scr[hl, n], sb, preferred_element_type=F32) + b_scr[hl, n]
            v_new = u_scr[hl, rows, :] - ws[:CHUNK]
            o = ws[CHUNK:] + jnp.dot(qk_scr[hl, rows, :], v_new.astype(BF16), preferred_element_type=F32)
            on = o * lax.rsqrt(jnp.mean(o * o, axis=-1, keepdims=True) + NORM_EPS) * nw
            o_ref[rows, cols] = (on * _silu(z_ref[rows, cols])).astype(o_ref.dtype)
    for hl in heads:
        s_scr[hp * GDN_HPS + hl] = states[hl]


def _gated_deltanet(proj, gates, conv_w, a_log, dt_bias, norm_w):
    t = proj.shape[0]
    tb = GDN_TB
    nh = GDN_HEADS
    w = GDN_HPS * LANES
    nb = GDN_QK // w
    pad = lambda a: jnp.zeros((1, LANES), F32).at[0, nh:2 * nh].set(a.astype(F32))
    cur = lambda sec: pl.BlockSpec((tb, w), lambda i, h: (i, sec * nb + h))
    prev = lambda sec: pl.BlockSpec((HALO, w), lambda i, h: (jnp.maximum(i * (tb // HALO) - 1, 0), sec * nb + h))
    cw = lambda sec: pl.BlockSpec((CONV_K, w), lambda i, h: (0, sec * nb + h))
    row = pl.BlockSpec((1, LANES), lambda i, h: (0, 0))
    return pl.pallas_call(
        _gdn_kernel,
        grid=(t // tb, nh // GDN_HPS),
        in_specs=[cur(0), prev(0), cur(1), prev(1), cur(2), prev(2), cw(0), cw(1), cw(2),
                  pl.BlockSpec((tb, LANES), lambda i, h: (i, 0)), row, row, cur(3), row],
        out_specs=pl.BlockSpec((tb, w), lambda i, h: (i, h)),
        out_shape=jax.ShapeDtypeStruct((t, GDN_V), BF16),
        scratch_shapes=[pltpu.VMEM((tb + HALO, w), F32)] * 3 + [
            pltpu.VMEM((tb, LANES), F32), pltpu.VMEM((tb, LANES), F32),
            pltpu.VMEM((tb // LANES, LANES, LANES), F32),
            pltpu.VMEM((GDN_HPS, GDN_NC, 2 * CHUNK, GDN_DK), BF16),
            pltpu.VMEM((GDN_HPS, tb, GDN_DV), F32),
            pltpu.VMEM((GDN_HPS, tb, CHUNK), BF16),
            pltpu.VMEM((GDN_HPS, GDN_NC, GDN_DK, GDN_DK), BF16),
            pltpu.VMEM((GDN_HPS, GDN_NC, GDN_DK, GDN_DV), F32),
            pltpu.VMEM((nh, GDN_DK, GDN_DV), F32)],
        compiler_params=_params(("arbitrary", "arbitrary")),
        name="gated_deltanet",
    )(proj, proj, proj, proj, proj, proj, conv_w, conv_w, conv_w,
      gates, pad(a_log), pad(dt_bias), proj, norm_w.reshape(1, GDN_DV))


SSD_TB = 512
SSD_NC = SSD_TB // CHUNK


def _ssd_expand_matrix():
    j = jnp.arange(LANES)[:, None]
    col = jnp.arange(2 * SSD_GW)[None, :]
    head = (col % SSD_GW) // SSD_HEADDIM
    piece = j // SSD_HPG
    is_ac = (col < SSD_GW) & (piece < 3)
    is_dt = (col >= SSD_GW) & (piece >= 3) & (piece < 5)
    return jnp.where((is_ac | is_dt) & (j % SSD_HPG == head), 1.0, 0.0).astype(BF16)


def _ssd_kernel(xc_ref, xp_ref, bc_ref, bp_ref, cc_ref, cp_ref, wx_ref, wb_ref, wc_ref,
                bx_ref, bb_ref, bcn_ref, dt_ref, dtb_ref, alog_ref, dsk_ref, z_ref, nw_ref, ex_ref, o_ref,
                cx_scr, cb_scr, cc_scr, dt_scr, ac_scr, x_scr, e_scr, st_scr):
    i = pl.program_id(0)
    g = pl.program_id(1)
    tb = SSD_TB
    hpg = SSD_HPG
    first = i == 0

    @pl.when(first)
    def _():
        st_scr[g] = jnp.zeros(st_scr.shape[1:], F32)

    @pl.when(g == 0)
    def _():
        dt = _softplus(dt_ref[...] + dtb_ref[...])
        dt_scr[...] = dt
        ac_scr[...] = _chunk_cumsum(dt * (-jnp.exp(alog_ref[...])))

    shift = (LANES - g * hpg) % LANES
    lane = lax.broadcasted_iota(jnp.int32, (1, LANES), 1)
    mine = lane < hpg
    dt_g = jnp.where(mine, pltpu.roll(dt_scr[...], shift, 1), 0.0)
    ac_g = jnp.where(mine, pltpu.roll(ac_scr[...], shift, 1), 0.0)

    a1, a2, a3 = _split3(ac_g)
    d1 = dt_g.astype(BF16)
    d2 = (dt_g - d1.astype(F32)).astype(BF16)
    packed = a1.astype(F32)
    for k, piece in enumerate((a2, a3, d1, d2), start=1):
        packed = packed + pltpu.roll(piece.astype(F32), k * hpg, 1)
    e_scr[...] = jnp.dot(packed.astype(BF16), ex_ref[...], preferred_element_type=F32)

    ac_t = [ac_g[s * LANES:(s + 1) * LANES, :].T[0:8, :] for s in range(tb // LANES)]
    ac_tr = [pltpu.roll(a, CHUNK, 1) for a in ac_t]

    x_scr[...] = _silu(_causal_conv(xc_ref, xp_ref, wx_ref, cx_scr, first) + bx_ref[...])
    bm = _silu(_causal_conv(bc_ref, bp_ref, wb_ref, cb_scr, first) + bb_ref[...])
    cm = _silu(_causal_conv(cc_ref, cp_ref, wc_ref, cc_scr, first) + bcn_ref[...])

    r = lax.broadcasted_iota(jnp.int32, (CHUNK, LANES), 0)
    c = lax.broadcasted_iota(jnp.int32, (CHUNK, LANES), 1)
    causal2 = r >= (c % CHUNK)
    left = c < CHUNK
    dsk = dsk_ref[...]
    nw = nw_ref[...]
    gw = SSD_GW

    chunks = range(SSD_NC)
    rows = [slice(n * CHUNK, (n + 1) * CHUNK) for n in chunks]
    cbs = [_dot_nt(cm[rows[n]], bm[rows[n]]) for n in chunks]
    y_diag, upd, dec = [], [], []
    for n in chunks:
        acx = e_scr[rows[n], 0:gw]
        xdt = x_scr[rows[n], :] * e_scr[rows[n], gw:2 * gw]
        a_last = acx[CHUNK - 1:CHUNK, :]
        cb2 = jnp.concatenate([cbs[n], cbs[n]], axis=1)
        t_lo, t_hi = (ac_t[n // 2], ac_tr[n // 2]) if n % 2 == 0 else (ac_tr[n // 2], ac_t[n // 2])
        ys = []
        for p in range(hpg // 2):
            lanes = slice(p * LANES, (p + 1) * LANES)
            a_row = jnp.where(lane < CHUNK, t_lo[2 * p:2 * p + 1, :], t_hi[2 * p + 1:2 * p + 2, :])
            decay = jnp.where(causal2, jnp.exp(jnp.where(causal2, acx[:, lanes] - a_row, 0.0)), 0.0)
            xp = xdt[:, lanes]
            x_bd = jnp.concatenate([jnp.where(left, xp, 0.0), jnp.where(left, 0.0, xp)], axis=0)
            ys.append(_dot(cb2 * decay, x_bd))
        y_diag.append(jnp.concatenate(ys, axis=1))
        upd.append(_dot_tn(bm[rows[n]], xdt * jnp.exp(a_last - acx)))
        dec.append(jnp.exp(a_last))

    state = st_scr[g]
    states = []
    for n in chunks:
        states.append(state.astype(BF16))
        state = state * dec[n] + upd[n]
    st_scr[g] = state

    for n in chunks:
        acx = e_scr[rows[n], 0:gw]
        y_off = jnp.dot(cm[rows[n]].astype(BF16), states[n], preferred_element_type=F32)
        y = y_diag[n] + y_off * jnp.exp(acx) + x_scr[rows[n], :] * dsk
        y = y * _silu(z_ref[rows[n], :])
        y = y * lax.rsqrt(jnp.mean(y * y, axis=-1, keepdims=True) + NORM_EPS) * nw
        o_ref[rows[n], :] = y.astype(o_ref.dtype)


def _mamba2_ssd(proj, dt_raw, conv_w, conv_b, dt_bias, a_log, d_skip, norm_w):
    t = proj.shape[0]
    tb = SSD_TB
    gw = SSD_GW
    ng = SSD_GROUPS
    x0 = SSD_DINNER // gw
    b0 = 2 * SSD_DINNER // LANES
    c0 = b0 + ng
    pad = lambda a: jnp.zeros((1, LANES), F32).at[0, :SSD_HEADS].set(a.astype(F32))
    prev_idx = lambda i: jnp.maximum(i * (tb // HALO) - 1, 0)
    cbias = conv_b.reshape(1, -1)
    dsk = jnp.repeat(d_skip.astype(F32), SSD_HEADDIM).reshape(1, SSD_DINNER)
    return pl.pallas_call(
        _ssd_kernel,
        grid=(t // tb, ng),
        in_specs=[pl.BlockSpec((tb, gw), lambda i, g: (i, x0 + g)),
                  pl.BlockSpec((HALO, gw), lambda i, g: (prev_idx(i), x0 + g)),
                  pl.BlockSpec((tb, LANES), lambda i, g: (i, b0 + g)),
                  pl.BlockSpec((HALO, LANES), lambda i, g: (prev_idx(i), b0 + g)),
                  pl.BlockSpec((tb, LANES), lambda i, g: (i, c0 + g)),
                  pl.BlockSpec((HALO, LANES), lambda i, g: (prev_idx(i), c0 + g)),
                  pl.BlockSpec((CONV_K, gw), lambda i, g: (0, g)),
                  pl.BlockSpec((CONV_K, LANES), lambda i, g: (0, SSD_DINNER // LANES + g)),
                  pl.BlockSpec((CONV_K, LANES), lambda i, g: (0, SSD_DINNER // LANES + ng + g)),
                  pl.BlockSpec((1, gw), lambda i, g: (0, g)),
                  pl.BlockSpec((1, LANES), lambda i, g: (0, SSD_DINNER // LANES + g)),
                  pl.BlockSpec((1, LANES), lambda i, g: (0, SSD_DINNER // LANES + ng + g)),
                  pl.BlockSpec((tb, LANES), lambda i, g: (i, 0)),
                  pl.BlockSpec((1, LANES), lambda i, g: (0, 0)),
                  pl.BlockSpec((1, LANES), lambda i, g: (0, 0)),
                  pl.BlockSpec((1, gw), lambda i, g: (0, g)),
                  pl.BlockSpec((tb, gw), lambda i, g: (i, g)),
                  pl.BlockSpec((1, gw), lambda i, g: (0, g)),
                  pl.BlockSpec((LANES, 2 * gw), lambda i, g: (0, 0))],
        out_specs=pl.BlockSpec((tb, gw), lambda i, g: (i, g)),
        out_shape=jax.ShapeDtypeStruct((t, SSD_DINNER), BF16),
        scratch_shapes=[pltpu.VMEM((tb + HALO, gw), F32), pltpu.VMEM((tb + HALO, LANES), F32),
                        pltpu.VMEM((tb + HALO, LANES), F32),
                        pltpu.VMEM((tb, LANES), F32), pltpu.VMEM((tb, LANES), F32),
                        pltpu.VMEM((tb, gw), F32), pltpu.VMEM((tb, 2 * gw), F32),
                        pltpu.VMEM((ng, SSD_DSTATE, gw), F32)],
        compiler_params=_params(("arbitrary", "arbitrary")),
        name="mamba2_ssd",
    )(proj, proj, proj, proj, proj, proj, conv_w, conv_w, conv_w, cbias, cbias, cbias,
      dt_raw, pad(dt_bias), pad(a_log), dsk, proj, norm_w.reshape(1, SSD_DINNER), _ssd_expand_matrix())


def _layer_mods(mod, layer):
    d = D_MODEL
    return [mod[layer, k * d:(k + 1) * d].reshape(1, d) for k in range(6)]


def kernel(x, c, mod_w, mod_b, norm_mix_w, norm_mlp_w, mlp_w1, mlp_w2, ab_w_in, gdn_conv_w, gdn_a_log,
           gdn_dt_bias, gdn_norm_w, attn_q_norm_w, attn_k_norm_w, attn_rel_bias, ab_w_out, ssd_w_in,
           ssd_conv_w, ssd_conv_b, ssd_dt_bias, ssd_a_log, ssd_d, ssd_norm_w, ssd_w_out):
    b, t, d = x.shape
    assert b == 1 and d == D_MODEL
    xs = x.reshape(t, d)
    mod = _modulation(c, mod_w, mod_b)
    w1_b, w2_b = mlp_w1.astype(BF16), mlp_w2.astype(BF16)

    sh1, sc1, g1, sh2, sc2, g2 = _layer_mods(mod, 0)
    w_in = ab_w_in[0]
    gate_lo = 2 * GDN_QK + 2 * GDN_V
    gate_hi = gate_lo + 2 * GDN_HEADS
    w_in_b = w_in.astype(BF16)
    w_gate = jnp.pad(w_in[:, gate_lo:gate_hi], ((0, 0), (0, LANES - 2 * GDN_HEADS)))
    proj, gates = _norm_proj2(xs, norm_mix_w[0].reshape(1, d), sc1, sh1, w_in_b, gate_lo, w_in_b[:, gate_hi:],
                              w_gate, tm=256)
    o_a = _gated_deltanet(proj, gates, gdn_conv_w[0], gdn_a_log[0], gdn_dt_bias[0], gdn_norm_w[0])
    a0 = gate_lo // ATT_W
    o_b = _band_attention(proj, a0, a0 + 1, a0 + 2, attn_q_norm_w[0], attn_k_norm_w[0], attn_rel_bias[0])
    xs, h2 = _out_proj([o_a, o_b], [ab_w_out[0][:GDN_V], ab_w_out[0][GDN_V:]], xs, g1,
                       norm_mlp_w[0].reshape(1, d), sc2, sh2)
    xs = _mlp(xs, h2, g2, w1_b, w2_b, 0)

    sh1, sc1, g1, sh2, sc2, g2 = _layer_mods(mod, 1)
    w_in = ssd_w_in[0]
    n_main = 2 * SSD_DINNER + 2 * SSD_GROUPS * SSD_DSTATE
    w_dt = jnp.pad(w_in[:, n_main:], ((0, 0), (0, LANES - SSD_HEADS)))
    proj, dt_raw = _norm_proj(xs, norm_mix_w[1].reshape(1, d), sc1, sh1, w_in.astype(BF16), n_main, w_dt, tm=256)
    y = _mamba2_ssd(proj, dt_raw, ssd_conv_w[0], ssd_conv_b[0], ssd_dt_bias[0], ssd_a_log[0],
                    ssd_d[0], ssd_norm_w[0])
    xs, h2 = _out_proj([y], [ssd_w_out[0]], xs, g1, norm_mlp_w[1].reshape(1, d), sc2, sh2)
    xs = _mlp(xs, h2, g2, w1_b, w2_b, 1)
    return xs.reshape(b, t, d)
```

```python
import functools

import jax
import jax.numpy as jnp
from jax import lax
from jax.experimental import pallas as pl
from jax.experimental.pallas import tpu as pltpu

F32 = jnp.float32
BF16 = jnp.bfloat16

D_MODEL = 2048
CHUNK = 64
NORM_EPS = 1e-6
CONV_K = 4
HALO = 8
GDN_HEADS = 8
GDN_DK = 128
GDN_DV = 128
GDN_QK = GDN_HEADS * GDN_DK
GDN_V = GDN_HEADS * GDN_DV
ATT_HEADS = 8
ATT_DH = 128
ATT_W = ATT_HEADS * ATT_DH
BAND_CHUNKS = 9
REL_CLIP = 256
SSD_DINNER = 2 * D_MODEL
SSD_HEADDIM = 64
SSD_HEADS = SSD_DINNER // SSD_HEADDIM
SSD_GROUPS = 8
SSD_HPG = SSD_HEADS // SSD_GROUPS
SSD_DSTATE = 128
SSD_GW = SSD_HPG * SSD_HEADDIM
D_FF = 4 * D_MODEL
LANES = 128
NEG_BIG = -1e30
LOG2E = 1.4426950408889634

VMEM_LIMIT = 56 * 1024 * 1024

_NT = (((1,), (1,)), ((), ()))
_TN = (((0,), (0,)), ((), ()))


def _dot(a, b):
    return jnp.dot(a.astype(BF16), b.astype(BF16), preferred_element_type=F32)


def _dot_nt(a, b):
    return lax.dot_general(a.astype(BF16), b.astype(BF16), _NT, preferred_element_type=F32)


def _dot_tn(a, b):
    return lax.dot_general(a.astype(BF16), b.astype(BF16), _TN, preferred_element_type=F32)


def _split3(a):
    a1 = a.astype(BF16)
    r1 = a - a1.astype(F32)
    a2 = r1.astype(BF16)
    a3 = (r1 - a2.astype(F32)).astype(BF16)
    return a1, a2, a3


def _dot_exact_lhs(a_bf16, b):
    b1, b2, b3 = _split3(b)
    out = jnp.dot(a_bf16, b1, preferred_element_type=F32)
    out += jnp.dot(a_bf16, b2, preferred_element_type=F32)
    out += jnp.dot(a_bf16, b3, preferred_element_type=F32)
    return out


def _silu(x):
    return x * jax.nn.sigmoid(x)


def _softplus(x):
    return jnp.maximum(x, 0.0) + jnp.log(1.0 + jnp.exp(-jnp.abs(x)))


def _norm_mod(x, nw, sc, sh):
    ms = jnp.mean(x * x, axis=-1, keepdims=True)
    return (x * lax.rsqrt(ms + NORM_EPS) * nw) * (1.0 + sc) + sh


def _seg_tri(n):
    r = lax.broadcasted_iota(jnp.int32, (n, n), 0)
    c = lax.broadcasted_iota(jnp.int32, (n, n), 1)
    return jnp.where((r >= c) & ((r // CHUNK) == (c // CHUNK)), 1.0, 0.0).astype(BF16)


MXU_K = 256


def _chunk_cumsum(g):
    tri = _seg_tri(MXU_K)
    return jnp.concatenate([_dot_exact_lhs(tri, g[s:s + MXU_K]) for s in range(0, g.shape[0], MXU_K)], axis=0)


def _params(sem):
    return pltpu.CompilerParams(dimension_semantics=sem, vmem_limit_bytes=VMEM_LIMIT)


def _mod_kernel(c_ref, w_ref, b_ref, o_ref):
    c = c_ref[...]
    o_ref[...] = jnp.dot(_silu(c), w_ref[...], preferred_element_type=F32,
                         precision=lax.Precision.HIGHEST) + b_ref[...]


def _modulation(c, mod_w, mod_b, tn=1024):
    depth, d, n = mod_w.shape
    c8 = jnp.broadcast_to(c.reshape(1, d), (8, d))
    out = pl.pallas_call(
        _mod_kernel,
        grid=(depth, n // tn),
        in_specs=[pl.BlockSpec((8, d), lambda l, j: (0, 0)),
                  pl.BlockSpec((None, d, tn), lambda l, j: (l, 0, j)),
                  pl.BlockSpec((None, 1, tn), lambda l, j: (l, 0, j))],
        out_specs=pl.BlockSpec((None, 8, tn), lambda l, j: (l, 0, j)),
        out_shape=jax.ShapeDtypeStruct((depth, 8, n), F32),
        compiler_params=_params(("arbitrary", "arbitrary")),
        name="modulation",
    )(c8, mod_w, mod_b.reshape(depth, 1, n))
    return out[:, 0, :]


def _proj_kernel(x_ref, nw_ref, sc_ref, sh_ref, w_ref, wg_ref, o_ref, g_ref):
    h = _norm_mod(x_ref[...], nw_ref[...], sc_ref[...], sh_ref[...])
    hb = h.astype(BF16)
    o_ref[...] = jnp.dot(hb, w_ref[...], preferred_element_type=F32)
    ng = g_ref.shape[-1]

    @pl.when(pl.program_id(0) == 0)
    def _():
        r = jnp.dot(hb, wg_ref[...], preferred_element_type=F32)
        g_ref[...] = r[:, :ng] + r[:, ng:]

    @pl.when(pl.program_id(0) != 0)
    def _():
        g_ref[...] = jnp.zeros_like(g_ref)


def _cast_kernel(g0, gw, w_ref, o_ref, g_ref):
    w = w_ref[...]
    o_ref[...] = w.astype(o_ref.dtype)
    g_ref[...] = w[:, g0:g0 + gw]


def _cast_weight(w3, g0, gw, tr=256):
    _, d, n = w3.shape
    return pl.pallas_call(
        functools.partial(_cast_kernel, g0, gw),
        grid=(d // tr,),
        in_specs=[pl.BlockSpec((None, tr, n), lambda i: (0, i, 0))],
        out_specs=[pl.BlockSpec((tr, n), lambda i: (i, 0)), pl.BlockSpec((tr, gw), lambda i: (i, 0))],
        out_shape=[jax.ShapeDtypeStruct((d, n), BF16), jax.ShapeDtypeStruct((d, gw), F32)],
        compiler_params=_params(("arbitrary",)),
        name="cast_weight",
    )(w3)


def _split_gate_weights(wg):
    wg1 = wg.astype(BF16)
    wg2 = (wg - wg1.astype(F32)).astype(BF16)
    return jnp.concatenate([wg1, wg2], axis=1)


def _norm_proj(x, nw, sc, sh, w_bf16, n, wg, tm, n_split=2):
    t, d = x.shape
    ng = wg.shape[1]
    tn = n // n_split
    row = lambda s, i: (0, 0)
    out, gates = pl.pallas_call(
        _proj_kernel,
        grid=(n_split, t // tm),
        in_specs=[pl.BlockSpec((tm, d), lambda s, i: (i, 0)),
                  pl.BlockSpec((1, d), row), pl.BlockSpec((1, d), row), pl.BlockSpec((1, d), row),
                  pl.BlockSpec((d, tn), lambda s, i: (0, s), pipeline_mode=pl.Buffered(1)),
                  pl.BlockSpec((d, 2 * ng), row, pipeline_mode=pl.Buffered(1))],
        out_specs=[pl.BlockSpec((tm, tn), lambda s, i: (i, s)),
                   pl.BlockSpec((None, tm, ng), lambda s, i: (s, i, 0))],
        out_shape=[jax.ShapeDtypeStruct((t, n), F32), jax.ShapeDtypeStruct((n_split, t, ng), F32)],
        compiler_params=_params(("arbitrary", "arbitrary")),
        name="norm_proj",
    )(x, nw, sc, sh, w_bf16, _split_gate_weights(wg))
    return out, gates[0]


def _proj2_kernel(x_ref, nw_ref, sc_ref, sh_ref, wa_ref, wb_ref, wg_ref, o_ref, g_ref):
    h = _norm_mod(x_ref[...], nw_ref[...], sc_ref[...], sh_ref[...])
    hb = h.astype(BF16)
    na = wa_ref.shape[1]
    o_ref[:, :na] = jnp.dot(hb, wa_ref[...], preferred_element_type=F32)
    o_ref[:, na:] = jnp.dot(hb, wb_ref[...], preferred_element_type=F32)
    ng = g_ref.shape[-1]
    r = jnp.dot(hb, wg_ref[...], preferred_element_type=F32)
    g_ref[...] = r[:, :ng] + r[:, ng:]


def _norm_proj2(x, nw, sc, sh, wa_bf16, na, wb_bf16, wg, tm):
    t, d = x.shape
    nb = wb_bf16.shape[1]
    ng = wg.shape[1]
    row = lambda i: (0, 0)
    return pl.pallas_call(
        _proj2_kernel,
        grid=(t // tm,),
        in_specs=[pl.BlockSpec((tm, d), lambda i: (i, 0)),
                  pl.BlockSpec((1, d), row), pl.BlockSpec((1, d), row), pl.BlockSpec((1, d), row),
                  pl.BlockSpec((d, na), row, pipeline_mode=pl.Buffered(1)),
                  pl.BlockSpec((d, nb), row, pipeline_mode=pl.Buffered(1)),
                  pl.BlockSpec((d, 2 * ng), row, pipeline_mode=pl.Buffered(1))],
        out_specs=[pl.BlockSpec((tm, na + nb), lambda i: (i, 0)), pl.BlockSpec((tm, ng), lambda i: (i, 0))],
        out_shape=[jax.ShapeDtypeStruct((t, na + nb), F32), jax.ShapeDtypeStruct((t, ng), F32)],
        compiler_params=_params(("arbitrary",)),
        name="norm_proj2",
    )(x, nw, sc, sh, wa_bf16, wb_bf16, _split_gate_weights(wg))


def _out_kernel(n_in, *refs):
    y_refs = refs[:n_in]
    w_refs = refs[n_in:2 * n_in]
    x_ref, g_ref, nw_ref, sc_ref, sh_ref, o_ref, h_ref = refs[2 * n_in:]
    acc = jnp.dot(y_refs[0][...], w_refs[0][...], preferred_element_type=F32)
    for y_ref, w_ref in zip(y_refs[1:], w_refs[1:]):
        acc += jnp.dot(y_ref[...], w_ref[...], preferred_element_type=F32)
    x_new = x_ref[...] + g_ref[...] * acc
    o_ref[...] = x_new
    h_ref[...] = _norm_mod(x_new, nw_ref[...], sc_ref[...], sh_ref[...]).astype(h_ref.dtype)


def _out_proj(ys, ws, x, gate, nw, sc, sh, tm=512):
    t, d = x.shape
    n_in = len(ys)
    row = pl.BlockSpec((1, d), lambda i: (0, 0))
    in_specs = [pl.BlockSpec((tm, y.shape[1]), lambda i: (i, 0)) for y in ys]
    in_specs += [pl.BlockSpec(w.shape, lambda i: (0, 0), pipeline_mode=pl.Buffered(1)) for w in ws]
    in_specs += [pl.BlockSpec((tm, d), lambda i: (i, 0)), row, row, row, row]
    return pl.pallas_call(
        functools.partial(_out_kernel, n_in),
        grid=(t // tm,),
        in_specs=in_specs,
        out_specs=[pl.BlockSpec((tm, d), lambda i: (i, 0)), pl.BlockSpec((tm, d), lambda i: (i, 0))],
        out_shape=[jax.ShapeDtypeStruct((t, d), F32), jax.ShapeDtypeStruct((t, d), BF16)],
        compiler_params=_params(("arbitrary",)),
        name="out_proj",
    )(*ys, *[w.astype(BF16) for w in ws], x, gate, nw, sc, sh)


def _mlp_kernel(x_ref, h_ref, g_ref, w1_ref, w2_ref, o_ref):
    f = pl.program_id(1)

    @pl.when(f == 0)
    def _():
        o_ref[...] = jnp.zeros_like(o_ref)

    a = jnp.maximum(jnp.dot(h_ref[...], w1_ref[...], preferred_element_type=F32), 0.0)
    o_ref[...] += jnp.dot((a * a).astype(BF16), w2_ref[...], preferred_element_type=F32)

    @pl.when(f == pl.num_programs(1) - 1)
    def _():
        o_ref[...] = x_ref[...] + g_ref[...] * o_ref[...]


def _mlp(x, h, gate, w1_all, w2_all, layer, tm=1024, tf=512):
    t, d = x.shape
    ff = w1_all.shape[2]
    return pl.pallas_call(
        _mlp_kernel,
        grid=(t // tm, ff // tf),
        in_specs=[pl.BlockSpec((tm, d), lambda i, f: (i, 0)),
                  pl.BlockSpec((tm, d), lambda i, f: (i, 0)),
                  pl.BlockSpec((1, d), lambda i, f: (0, 0)),
                  pl.BlockSpec((None, d, tf), lambda i, f: (layer, 0, f)),
                  pl.BlockSpec((None, tf, d), lambda i, f: (layer, f, 0))],
        out_specs=pl.BlockSpec((tm, d), lambda i, f: (i, 0)),
        out_shape=jax.ShapeDtypeStruct((t, d), F32),
        compiler_params=_params(("parallel", "arbitrary")),
        name="mlp",
    )(x, h, gate, w1_all, w2_all)


ATT_TQ = 256
ATT_KB = 3
ATT_HG = 4


def _attn_kernel(q_ref, k_ref, v_ref, bias_ref, qw_ref, kw_ref, o_ref, kn_scr, v_scr):
    i = pl.program_id(0)

    @pl.when(i == 0)
    def _():
        kn_scr[...] = jnp.zeros_like(kn_scr)
        v_scr[...] = jnp.zeros_like(v_scr)

    def rms(x, w):
        return x * lax.rsqrt(jnp.mean(x * x, axis=-1, keepdims=True) + NORM_EPS) * w

    slots = [lax.rem(i + 1 + b, ATT_KB) for b in range(ATT_KB)]
    kw = kw_ref[...]
    qw = qw_ref[...]
    for h in range(ATT_HEADS):
        cols = slice(h * ATT_DH, (h + 1) * ATT_DH)
        kn_scr[slots[-1], :, cols] = rms(k_ref[:, cols], kw).astype(BF16)
    v_scr[slots[-1]] = v_ref[...].astype(BF16)

    for h0 in range(0, ATT_HEADS, ATT_HG):
        hs = range(h0, h0 + ATT_HG)
        col = {h: slice(h * ATT_DH, (h + 1) * ATT_DH) for h in hs}
        q = {h: (rms(q_ref[:, col[h]], qw) * (ATT_DH ** -0.5 * LOG2E)).astype(BF16) for h in hs}
        s = {}
        for h in hs:
            for b in range(ATT_KB):
                sb = lax.dot_general(q[h], kn_scr[slots[b], :, col[h]], _NT, preferred_element_type=F32)
                sb = sb + bias_ref[h, :, b * ATT_TQ:(b + 1) * ATT_TQ]
                s[h, b] = jnp.where(i >= ATT_KB - 1 - b, sb, NEG_BIG)
        p, l = {}, {}
        for h in hs:
            m = jnp.max(jnp.maximum(jnp.maximum(s[h, 0], s[h, 1]), s[h, 2]), axis=-1, keepdims=True)
            for b in range(ATT_KB):
                p[h, b] = jnp.exp2(s[h, b] - m)
            l[h] = jnp.sum(p[h, 0] + p[h, 1] + p[h, 2], axis=-1, keepdims=True)
        for h in hs:
            o = jnp.dot(p[h, 0].astype(BF16), v_scr[slots[0], :, col[h]], preferred_element_type=F32)
            for b in range(1, ATT_KB):
                o += jnp.dot(p[h, b].astype(BF16), v_scr[slots[b], :, col[h]], preferred_element_type=F32)
            o_ref[:, col[h]] = (o / l[h]).astype(o_ref.dtype)


def _attn_bias_table(rel_bias):
    nk = ATT_KB * ATT_TQ
    span = ATT_TQ + nk
    dist = jnp.arange(span) - (ATT_TQ - 1) - (ATT_KB - 1) * ATT_TQ
    ext = rel_bias.astype(F32)[:, jnp.clip(dist, -REL_CLIP, REL_CLIP) + REL_CLIP]
    flat = jnp.tile(ext, (1, ATT_TQ))[:, :ATT_TQ * (span - 1)]
    tab = flat.reshape(-1, ATT_TQ, span - 1)[:, :, ATT_TQ - 1:ATT_TQ - 1 + nk]
    r = jnp.arange(ATT_TQ)[:, None]
    m = jnp.arange(nk)[None, :]
    qc = r // CHUNK
    kc = m // CHUNK
    in_band = (kc >= qc) & (kc <= qc + BAND_CHUNKS - 1)
    return jnp.where(in_band[None], tab * LOG2E, NEG_BIG)


def _band_attention(proj, q_blk, k_blk, v_blk, q_norm_w, k_norm_w, rel_bias):
    t = proj.shape[0]
    bias = _attn_bias_table(rel_bias)
    blk = (ATT_TQ, ATT_W)
    return pl.pallas_call(
        _attn_kernel,
        grid=(t // ATT_TQ,),
        in_specs=[pl.BlockSpec(blk, lambda i: (i, q_blk)),
                  pl.BlockSpec(blk, lambda i: (i, k_blk)),
                  pl.BlockSpec(blk, lambda i: (i, v_blk)),
                  pl.BlockSpec(bias.shape, lambda i: (0, 0, 0), pipeline_mode=pl.Buffered(1)),
                  pl.BlockSpec((1, ATT_DH), lambda i: (0, 0)),
                  pl.BlockSpec((1, ATT_DH), lambda i: (0, 0))],
        out_specs=pl.BlockSpec(blk, lambda i: (i, 0)),
        out_shape=jax.ShapeDtypeStruct((t, ATT_W), BF16),
        scratch_shapes=[pltpu.VMEM((ATT_KB, ATT_TQ, ATT_W), BF16), pltpu.VMEM((ATT_KB, ATT_TQ, ATT_W), BF16)],
        compiler_params=_params(("arbitrary",)),
        name="band_attention",
    )(proj, proj, proj, bias, q_norm_w.reshape(1, ATT_DH), k_norm_w.reshape(1, ATT_DH))


GDN_TB = 1024
GDN_NC = GDN_TB // CHUNK
GDN_HPS = 4
GDN_GRP = 4


def _causal_conv(cur_ref, prev_ref, w_ref, scr, first_block):
    rows = cur_ref.shape[0]
    prev = prev_ref[...]
    scr[0:HALO, :] = jnp.where(first_block, jnp.zeros_like(prev), prev)
    scr[HALO:HALO + rows, :] = cur_ref[...]
    w = w_ref[...]
    ext = scr[...]
    acc = ext[HALO:, :] * w[CONV_K - 1:CONV_K, :]
    for s in range(1, CONV_K):
        acc += pltpu.roll(ext, s, 0)[HALO:, :] * w[CONV_K - 1 - s:CONV_K - s, :]
    return acc


def _block_diag(h):
    blk = lax.broadcasted_iota(jnp.int32, (1, h.shape[1]), 1) // CHUNK
    return jnp.concatenate([jnp.where(blk == j, h, jnp.zeros_like(h)) for j in range(h.shape[1] // CHUNK)], axis=0)


def _gdn_kernel(qc_ref, qp_ref, kc_ref, kp_ref, vc_ref, vp_ref, wq_ref, wk_ref, wv_ref,
                gate_ref, alog_ref, dtb_ref, z_ref, nw_ref, o_ref,
                cq_scr, ck_scr, cv_scr, beta_scr, gcum_scr, gt_scr,
                wq_scr, u_scr, qk_scr, a_scr, b_scr, s_scr):
    i = pl.program_id(0)
    hp = pl.program_id(1)
    tb = GDN_TB
    first = i == 0

    @pl.when(first)
    def _():
        for hl in range(GDN_HPS):
            s_scr[hp * GDN_HPS + hl] = jnp.zeros(s_scr.shape[1:], F32)

    @pl.when(hp == 0)
    def _():
        gate = gate_ref[...]
        beta_scr[...] = jax.nn.sigmoid(gate)
        g = -jnp.exp(alog_ref[...]) * _softplus(gate + dtb_ref[...]) * LOG2E
        gcum = _chunk_cumsum(g)
        gcum_scr[...] = gcum
        for s in range(tb // LANES):
            gt_scr[s] = gcum[s * LANES:(s + 1) * LANES, :].T

    lane = lax.broadcasted_iota(jnp.int32, (1, LANES), 1)
    r = lax.broadcasted_iota(jnp.int32, (CHUNK, CHUNK), 0)
    c = lax.broadcasted_iota(jnp.int32, (CHUNK, CHUNK), 1)
    causal = r >= c
    strict = r > c
    rh = lax.broadcasted_iota(jnp.int32, (CHUNK, GDN_GRP * CHUNK), 0)
    ch = lax.broadcasted_iota(jnp.int32, (CHUNK, GDN_GRP * CHUNK), 1)
    eye_h = jnp.where(rh == ch % CHUNK, 1.0, 0.0)
    nw = nw_ref[...]
    heads = range(GDN_HPS)
    groups = range(GDN_NC // GDN_GRP)
    gw = GDN_GRP * CHUNK

    def l2n(x):
        return x * lax.rsqrt(jnp.sum(x * x, axis=-1, keepdims=True) + NORM_EPS)

    qa = _silu(_causal_conv(qc_ref, qp_ref, wq_ref, cq_scr, first))
    ka = _silu(_causal_conv(kc_ref, kp_ref, wk_ref, ck_scr, first))
    va = _silu(_causal_conv(vc_ref, vp_ref, wv_ref, cv_scr, first))

    gcs, ks, rhs, low_hs, p_hs, x_hs = {}, {}, {}, {}, {}, {}
    for hl in heads:
        h = hp * GDN_HPS + hl
        cols = slice(hl * LANES, (hl + 1) * LANES)
        beta = jnp.sum(jnp.where(lane == h, beta_scr[...], 0.0), axis=1, keepdims=True)
        gc = jnp.sum(jnp.where(lane == GDN_HEADS + h, gcum_scr[...], 0.0), axis=1, keepdims=True)
        q = l2n(qa[:, cols]) * (GDN_DK ** -0.5)
        k = l2n(ka[:, cols])
        eg = jnp.exp2(gc)
        kb = k * beta
        rhs[hl] = jnp.concatenate([va[:, cols] * beta, kb * eg], axis=1).astype(BF16)
        qd = (q * eg).astype(BF16)
        gcs[hl], ks[hl] = gc, k
        for grp in groups:
            lows = []
            for j in range(GDN_GRP):
                n = grp * GDN_GRP + j
                rows = slice(n * CHUNK, (n + 1) * CHUNK)
                gc_n = gc[rows]
                grow = gt_scr[n // 2, pl.ds(GDN_HEADS + h, 1), :]
                grow = grow[:, (n % 2) * CHUNK:(n % 2 + 1) * CHUNK]
                decay = jnp.where(causal, jnp.exp2(jnp.where(causal, gc_n - grow, 0.0)), 0.0)
                ab = _dot_nt(jnp.concatenate([kb[rows], q[rows]], axis=0), k[rows])
                lows.append(jnp.where(strict, ab[:CHUNK] * decay, 0.0))
                qk_scr[hl, rows, :] = (ab[CHUNK:] * decay).astype(BF16)
                wq_scr[hl, n, CHUNK:2 * CHUNK, :] = qd[rows]
            low_hs[hl, grp] = jnp.concatenate(lows, axis=1)

    for key, low_h in low_hs.items():
        xb = low_h.astype(BF16)
        p_hs[key] = eye_h - low_h
        x_hs[key] = jnp.dot(xb, _block_diag(xb), preferred_element_type=F32)
    for step in range(5):
        for key in low_hs:
            xb = x_hs[key].astype(BF16)
            x_bd = _block_diag(xb)
            p_hs[key] = p_hs[key] + jnp.dot(p_hs[key].astype(BF16), x_bd, preferred_element_type=F32)
            if step < 4:
                x_hs[key] = jnp.dot(xb, x_bd, preferred_element_type=F32)

    for (hl, grp), p_h in p_hs.items():
        g0 = grp * gw
        uw = jnp.dot(_block_diag(p_h.astype(BF16)), rhs[hl][g0:g0 + gw], preferred_element_type=F32)
        u_scr[hl, g0:g0 + gw, :] = uw[:, :GDN_DV]
        for j in range(GDN_GRP):
            n = grp * GDN_GRP + j
            rows = slice(n * CHUNK, (n + 1) * CHUNK)
            uw_n = uw[j * CHUNK:(j + 1) * CHUNK]
            wq_scr[hl, n, 0:CHUNK, :] = uw_n[:, GDN_DV:].astype(BF16)
            gc_n = gcs[hl][rows]
            kd = ks[hl][rows] * jnp.exp2(gc_n[CHUNK - 1:CHUNK, :] - gc_n)
            ba = _dot_tn(kd, uw_n)
            b_scr[hl, n] = ba[:, :GDN_DV]
            a_scr[hl, n] = ba[:, GDN_DV:].astype(BF16)

    states = [s_scr[hp * GDN_HPS + hl] for hl in heads]
    for n in range(GDN_NC):
        rows = slice(n * CHUNK, (n + 1) * CHUNK)
        for hl in heads:
            cols = slice(hl * LANES, (hl + 1) * LANES)
            sb = states[hl].astype(BF16)
            ws = jnp.dot(wq_scr[hl, n], sb, preferred_element_type=F32)
            g_last = jnp.exp2(gcs[hl][(n + 1) * CHUNK - 1:(n + 1) * CHUNK, :])
            states[hl] = states[hl] * g_last - jnp.dot(a_scr[hl, n], sb, preferred_element_type=F32) + b_scr[hl, n]
            v_new = u_scr[hl, rows, :] - ws[:CHUNK]
            o = ws[CHUNK:] + jnp.dot(qk_scr[hl, rows, :], v_new.astype(BF16), preferred_element_type=F32)
            on = o * lax.rsqrt(jnp.mean(o * o, axis=-1, keepdims=True) + NORM_EPS) * nw
            o_ref[rows, cols] = (on * _silu(z_ref[rows, cols])).astype(o_ref.dtype)
    for hl in heads:
        s_scr[hp * GDN_HPS + hl] = states[hl]


def _gated_deltanet(proj, gates, conv_w, a_log, dt_bias, norm_w):
    t = proj.shape[0]
    tb = GDN_TB
    nh = GDN_HEADS
    w = GDN_HPS * LANES
    nb = GDN_QK // w
    pad = lambda a: jnp.zeros((1, LANES), F32).at[0, nh:2 * nh].set(a.astype(F32))
    cur = lambda sec: pl.BlockSpec((tb, w), lambda i, h: (i, sec * nb + h))
    prev = lambda sec: pl.BlockSpec((HALO, w), lambda i, h: (jnp.maximum(i * (tb // HALO) - 1, 0), sec * nb + h))
    cw = lambda sec: pl.BlockSpec((CONV_K, w), lambda i, h: (0, sec * nb + h))
    row = pl.BlockSpec((1, LANES), lambda i, h: (0, 0))
    return pl.pallas_call(
        _gdn_kernel,
        grid=(t // tb, nh // GDN_HPS),
        in_specs=[cur(0), prev(0), cur(1), prev(1), cur(2), prev(2), cw(0), cw(1), cw(2),
                  pl.BlockSpec((tb, LANES), lambda i, h: (i, 0)), row, row, cur(3), row],
        out_specs=pl.BlockSpec((tb, w), lambda i, h: (i, h)),
        out_shape=jax.ShapeDtypeStruct((t, GDN_V), BF16),
        scratch_shapes=[pltpu.VMEM((tb + HALO, w), F32)] * 3 + [
            pltpu.VMEM((tb, LANES), F32), pltpu.VMEM((tb, LANES), F32),
            pltpu.VMEM((tb // LANES, LANES, LANES), F32),
            pltpu.VMEM((GDN_HPS, GDN_NC, 2 * CHUNK, GDN_DK), BF16),
            pltpu.VMEM((GDN_HPS, tb, GDN_DV), F32),
            pltpu.VMEM((GDN_HPS, tb, CHUNK), BF16),
            pltpu.VMEM((GDN_HPS, GDN_NC, GDN_DK, GDN_DK), BF16),
            pltpu.VMEM((GDN_HPS, GDN_NC, GDN_DK, GDN_DV), F32),
            pltpu.VMEM((nh, GDN_DK, GDN_DV), F32)],
        compiler_params=_params(("arbitrary", "arbitrary")),
        name="gated_deltanet",
    )(proj, proj, proj, proj, proj, proj, conv_w, conv_w, conv_w,
      gates, pad(a_log), pad(dt_bias), proj, norm_w.reshape(1, GDN_DV))


SSD_TB = 512
SSD_NC = SSD_TB // CHUNK


def _ssd_expand_matrix():
    j = jnp.arange(LANES)[:, None]
    col = jnp.arange(2 * SSD_GW)[None, :]
    head = (col % SSD_GW) // SSD_HEADDIM
    piece = j // SSD_HPG
    is_ac = (col < SSD_GW) & (piece < 3)
    is_dt = (col >= SSD_GW) & (piece >= 3) & (piece < 5)
    return jnp.where((is_ac | is_dt) & (j % SSD_HPG == head), 1.0, 0.0).astype(BF16)


def _ssd_kernel(xc_ref, xp_ref, bc_ref, bp_ref, cc_ref, cp_ref, wx_ref, wb_ref, wc_ref,
                bx_ref, bb_ref, bcn_ref, dt_ref, dtb_ref, alog_ref, dsk_ref, z_ref, nw_ref, ex_ref, o_ref,
                cx_scr, cb_scr, cc_scr, dt_scr, ac_scr, x_scr, e_scr, st_scr):
    i = pl.program_id(0)
    g = pl.program_id(1)
    tb = SSD_TB
    hpg = SSD_HPG
    first = i == 0

    @pl.when(first)
    def _():
        st_scr[g] = jnp.zeros(st_scr.shape[1:], F32)

    @pl.when(g == 0)
    def _():
        dt = _softplus(dt_ref[...] + dtb_ref[...])
        dt_scr[...] = dt
        ac_scr[...] = _chunk_cumsum(dt * (-jnp.exp(alog_ref[...]) * LOG2E))

    shift = (LANES - g * hpg) % LANES
    lane = lax.broadcasted_iota(jnp.int32, (1, LANES), 1)
    mine = lane < hpg
    dt_g = jnp.where(mine, pltpu.roll(dt_scr[...], shift, 1), 0.0)
    ac_g = jnp.where(mine, pltpu.roll(ac_scr[...], shift, 1), 0.0)

    a1, a2, a3 = _split3(ac_g)
    d1 = dt_g.astype(BF16)
    d2 = (dt_g - d1.astype(F32)).astype(BF16)
    packed = a1.astype(F32)
    for k, piece in enumerate((a2, a3, d1, d2), start=1):
        packed = packed + pltpu.roll(piece.astype(F32), k * hpg, 1)
    e_scr[...] = jnp.dot(packed.astype(BF16), ex_ref[...], preferred_element_type=F32)

    ac_t = [ac_g[s * LANES:(s + 1) * LANES, :].T[0:8, :] for s in range(tb // LANES)]
    ac_tr = [pltpu.roll(a, CHUNK, 1) for a in ac_t]

    x_scr[...] = _silu(_causal_conv(xc_ref, xp_ref, wx_ref, cx_scr, first) + bx_ref[...])
    bm = _silu(_causal_conv(bc_ref, bp_ref, wb_ref, cb_scr, first) + bb_ref[...])
    cm = _silu(_causal_conv(cc_ref, cp_ref, wc_ref, cc_scr, first) + bcn_ref[...])

    r = lax.broadcasted_iota(jnp.int32, (CHUNK, LANES), 0)
    c = lax.broadcasted_iota(jnp.int32, (CHUNK, LANES), 1)
    causal2 = r >= (c % CHUNK)
    left = c < CHUNK
    dsk = dsk_ref[...]
    nw = nw_ref[...]
    gw = SSD_GW

    chunks = range(SSD_NC)
    rows = [slice(n * CHUNK, (n + 1) * CHUNK) for n in chunks]
    cbs = [_dot_nt(cm[rows[n]], bm[rows[n]]) for n in chunks]
    y_diag, upd, dec = [], [], []
    for n in chunks:
        acx = e_scr[rows[n], 0:gw]
        xdt = x_scr[rows[n], :] * e_scr[rows[n], gw:2 * gw]
        a_last = acx[CHUNK - 1:CHUNK, :]
        cb2 = jnp.concatenate([cbs[n], cbs[n]], axis=1)
        t_lo, t_hi = (ac_t[n // 2], ac_tr[n // 2]) if n % 2 == 0 else (ac_tr[n // 2], ac_t[n // 2])
        ys = []
        for p in range(hpg // 2):
            lanes = slice(p * LANES, (p + 1) * LANES)
            a_row = jnp.where(lane < CHUNK, t_lo[2 * p:2 * p + 1, :], t_hi[2 * p + 1:2 * p + 2, :])
            decay = jnp.where(causal2, jnp.exp2(jnp.where(causal2, acx[:, lanes] - a_row, 0.0)), 0.0)
            xp = xdt[:, lanes]
            x_bd = jnp.concatenate([jnp.where(left, xp, 0.0), jnp.where(left, 0.0, xp)], axis=0)
            ys.append(_dot(cb2 * decay, x_bd))
        y_diag.append(jnp.concatenate(ys, axis=1))
        upd.append(_dot_tn(bm[rows[n]], xdt * jnp.exp2(a_last - acx)))
        dec.append(jnp.exp2(a_last))

    state = st_scr[g]
    states = []
    for n in chunks:
        states.append(state.astype(BF16))
        state = state * dec[n] + upd[n]
    st_scr[g] = state

    for n in chunks:
        acx = e_scr[rows[n], 0:gw]
        y_off = jnp.dot(cm[rows[n]].astype(BF16), states[n], preferred_element_type=F32)
        y = y_diag[n] + y_off * jnp.exp2(acx) + x_scr[rows[n], :] * dsk
        y = y * _silu(z_ref[rows[n], :])
        y = y * lax.rsqrt(jnp.mean(y * y, axis=-1, keepdims=True) + NORM_EPS) * nw
        o_ref[rows[n], :] = y.astype(o_ref.dtype)


def _mamba2_ssd(proj, dt_raw, conv_w, conv_b, dt_bias, a_log, d_skip, norm_w):
    t = proj.shape[0]
    tb = SSD_TB
    gw = SSD_GW
    ng = SSD_GROUPS
    x0 = SSD_DINNER // gw
    b0 = 2 * SSD_DINNER // LANES
    c0 = b0 + ng
    pad = lambda a: jnp.zeros((1, LANES), F32).at[0, :SSD_HEADS].set(a.astype(F32))
    prev_idx = lambda i: jnp.maximum(i * (tb // HALO) - 1, 0)
    cbias = conv_b.reshape(1, -1)
    dsk = jnp.repeat(d_skip.astype(F32), SSD_HEADDIM).reshape(1, SSD_DINNER)
    return pl.pallas_call(
        _ssd_kernel,
        grid=(t // tb, ng),
        in_specs=[pl.BlockSpec((tb, gw), lambda i, g: (i, x0 + g)),
                  pl.BlockSpec((HALO, gw), lambda i, g: (prev_idx(i), x0 + g)),
                  pl.BlockSpec((tb, LANES), lambda i, g: (i, b0 + g)),
                  pl.BlockSpec((HALO, LANES), lambda i, g: (prev_idx(i), b0 + g)),
                  pl.BlockSpec((tb, LANES), lambda i, g: (i, c0 + g)),
                  pl.BlockSpec((HALO, LANES), lambda i, g: (prev_idx(i), c0 + g)),
                  pl.BlockSpec((CONV_K, gw), lambda i, g: (0, g)),
                  pl.BlockSpec((CONV_K, LANES), lambda i, g: (0, SSD_DINNER // LANES + g)),
                  pl.BlockSpec((CONV_K, LANES), lambda i, g: (0, SSD_DINNER // LANES + ng + g)),
                  pl.BlockSpec((1, gw), lambda i, g: (0, g)),
                  pl.BlockSpec((1, LANES), lambda i, g: (0, SSD_DINNER // LANES + g)),
                  pl.BlockSpec((1, LANES), lambda i, g: (0, SSD_DINNER // LANES + ng + g)),
                  pl.BlockSpec((tb, LANES), lambda i, g: (i, 0)),
                  pl.BlockSpec((1, LANES), lambda i, g: (0, 0)),
                  pl.BlockSpec((1, LANES), lambda i, g: (0, 0)),
                  pl.BlockSpec((1, gw), lambda i, g: (0, g)),
                  pl.BlockSpec((tb, gw), lambda i, g: (i, g)),
                  pl.BlockSpec((1, gw), lambda i, g: (0, g)),
                  pl.BlockSpec((LANES, 2 * gw), lambda i, g: (0, 0))],
        out_specs=pl.BlockSpec((tb, gw), lambda i, g: (i, g)),
        out_shape=jax.ShapeDtypeStruct((t, SSD_DINNER), BF16),
        scratch_shapes=[pltpu.VMEM((tb + HALO, gw), F32), pltpu.VMEM((tb + HALO, LANES), F32),
                        pltpu.VMEM((tb + HALO, LANES), F32),
                        pltpu.VMEM((tb, LANES), F32), pltpu.VMEM((tb, LANES), F32),
                        pltpu.VMEM((tb, gw), F32), pltpu.VMEM((tb, 2 * gw), F32),
                        pltpu.VMEM((ng, SSD_DSTATE, gw), F32)],
        compiler_params=_params(("arbitrary", "arbitrary")),
        name="mamba2_ssd",
    )(proj, proj, proj, proj, proj, proj, conv_w, conv_w, conv_w, cbias, cbias, cbias,
      dt_raw, pad(dt_bias), pad(a_log), dsk, proj, norm_w.reshape(1, SSD_DINNER), _ssd_expand_matrix())


def _layer_mods(mod, layer):
    d = D_MODEL
    return [mod[layer, k * d:(k + 1) * d].reshape(1, d) for k in range(6)]


def kernel(x, c, mod_w, mod_b, norm_mix_w, norm_mlp_w, mlp_w1, mlp_w2, ab_w_in, gdn_conv_w, gdn_a_log,
           gdn_dt_bias, gdn_norm_w, attn_q_norm_w, attn_k_norm_w, attn_rel_bias, ab_w_out, ssd_w_in,
           ssd_conv_w, ssd_conv_b, ssd_dt_bias, ssd_a_log, ssd_d, ssd_norm_w, ssd_w_out):
    b, t, d = x.shape
    assert b == 1 and d == D_MODEL
    xs = x.reshape(t, d)
    mod = _modulation(c, mod_w, mod_b)
    w1_b, w2_b = mlp_w1.astype(BF16), mlp_w2.astype(BF16)

    sh1, sc1, g1, sh2, sc2, g2 = _layer_mods(mod, 0)
    gate_lo = 2 * GDN_QK + 2 * GDN_V
    gate_hi = gate_lo + 2 * GDN_HEADS
    w_in_b, w_gate = _cast_weight(ab_w_in, gate_lo, 2 * GDN_HEADS)
    w_gate = jnp.pad(w_gate, ((0, 0), (0, LANES - 2 * GDN_HEADS)))
    proj, gates = _norm_proj2(xs, norm_mix_w[0].reshape(1, d), sc1, sh1, w_in_b, gate_lo, w_in_b[:, gate_hi:],
                              w_gate, tm=256)
    o_a = _gated_deltanet(proj, gates, gdn_conv_w[0], gdn_a_log[0], gdn_dt_bias[0], gdn_norm_w[0])
    a0 = gate_lo // ATT_W
    o_b = _band_attention(proj, a0, a0 + 1, a0 + 2, attn_q_norm_w[0], attn_k_norm_w[0], attn_rel_bias[0])
    xs, h2 = _out_proj([o_a, o_b], [ab_w_out[0][:GDN_V], ab_w_out[0][GDN_V:]], xs, g1,
                       norm_mlp_w[0].reshape(1, d), sc2, sh2)
    xs = _mlp(xs, h2, g2, w1_b, w2_b, 0)

    sh1, sc1, g1, sh2, sc2, g2 = _layer_mods(mod, 1)
    n_main = 2 * SSD_DINNER + 2 * SSD_GROUPS * SSD_DSTATE
    w_in_b, w_dt = _cast_weight(ssd_w_in, n_main, SSD_HEADS)
    w_dt = jnp.pad(w_dt, ((0, 0), (0, LANES - SSD_HEADS)))
    proj, dt_raw = _norm_proj(xs, norm_mix_w[1].reshape(1, d), sc1, sh1, w_in_b, n_main, w_dt, tm=256)
    y = _mamba2_ssd(proj, dt_raw, ssd_conv_w[0], ssd_conv_b[0], ssd_dt_bias[0], ssd_a_log[0],
                    ssd_d[0], ssd_norm_w[0])
    xs, h2 = _out_proj([y], [ssd_w_out[0]], xs, g1, norm_mlp_w[1].reshape(1, d), sc2, sh2)
    xs = _mlp(xs, h2, g2, w1_b, w2_b, 1)
    return xs.reshape(b, t, d)
```

```python
import functools

import jax
import jax.numpy as jnp
from jax import lax
from jax.experimental import pallas as pl
from jax.experimental.pallas import tpu as pltpu

F32 = jnp.float32
BF16 = jnp.bfloat16

D_MODEL = 2048
CHUNK = 64
NORM_EPS = 1e-6
CONV_K = 4
HALO = 8
GDN_HEADS = 8
GDN_DK = 128
GDN_DV = 128
GDN_QK = GDN_HEADS * GDN_DK
GDN_V = GDN_HEADS * GDN_DV
ATT_HEADS = 8
ATT_DH = 128
ATT_W = ATT_HEADS * ATT_DH
BAND_CHUNKS = 9
REL_CLIP = 256
SSD_DINNER = 2 * D_MODEL
SSD_HEADDIM = 64
SSD_HEADS = SSD_DINNER // SSD_HEADDIM
SSD_GROUPS = 8
SSD_HPG = SSD_HEADS // SSD_GROUPS
SSD_DSTATE = 128
SSD_GW = SSD_HPG * SSD_HEADDIM
D_FF = 4 * D_MODEL
LANES = 128
NEG_BIG = -1e30
LOG2E = 1.4426950408889634

VMEM_LIMIT = 56 * 1024 * 1024

_NT = (((1,), (1,)), ((), ()))
_TN = (((0,), (0,)), ((), ()))


def _dot(a, b):
    return jnp.dot(a.astype(BF16), b.astype(BF16), preferred_element_type=F32)


def _dot_nt(a, b):
    return lax.dot_general(a.astype(BF16), b.astype(BF16), _NT, preferred_element_type=F32)


def _dot_tn(a, b):
    return lax.dot_general(a.astype(BF16), b.astype(BF16), _TN, preferred_element_type=F32)


def _split3(a):
    a1 = a.astype(BF16)
    r1 = a - a1.astype(F32)
    a2 = r1.astype(BF16)
    a3 = (r1 - a2.astype(F32)).astype(BF16)
    return a1, a2, a3


def _dot_exact_lhs(a_bf16, b):
    b1, b2, b3 = _split3(b)
    out = jnp.dot(a_bf16, b1, preferred_element_type=F32)
    out += jnp.dot(a_bf16, b2, preferred_element_type=F32)
    out += jnp.dot(a_bf16, b3, preferred_element_type=F32)
    return out


def _silu(x):
    return x * jax.nn.sigmoid(x)


def _softplus(x):
    return jnp.maximum(x, 0.0) + jnp.log(1.0 + jnp.exp(-jnp.abs(x)))


def _norm_mod(x, nw, sc, sh):
    ms = jnp.mean(x * x, axis=-1, keepdims=True)
    return (x * lax.rsqrt(ms + NORM_EPS) * nw) * (1.0 + sc) + sh


def _seg_tri(n):
    r = lax.broadcasted_iota(jnp.int32, (n, n), 0)
    c = lax.broadcasted_iota(jnp.int32, (n, n), 1)
    return jnp.where((r >= c) & ((r // CHUNK) == (c // CHUNK)), 1.0, 0.0).astype(BF16)


MXU_K = 256


def _chunk_cumsum(g):
    tri = _seg_tri(MXU_K)
    return jnp.concatenate([_dot_exact_lhs(tri, g[s:s + MXU_K]) for s in range(0, g.shape[0], MXU_K)], axis=0)


def _params(sem):
    return pltpu.CompilerParams(dimension_semantics=sem, vmem_limit_bytes=VMEM_LIMIT)


def _mod_kernel(c_ref, w_ref, b_ref, o_ref):
    c = c_ref[...]
    o_ref[...] = jnp.dot(_silu(c), w_ref[...], preferred_element_type=F32,
                         precision=lax.Precision.HIGHEST) + b_ref[...]


def _modulation(c, mod_w, mod_b, tn=1024):
    depth, d, n = mod_w.shape
    c8 = jnp.broadcast_to(c.reshape(1, d), (8, d))
    out = pl.pallas_call(
        _mod_kernel,
        grid=(depth, n // tn),
        in_specs=[pl.BlockSpec((8, d), lambda l, j: (0, 0)),
                  pl.BlockSpec((None, d, tn), lambda l, j: (l, 0, j)),
                  pl.BlockSpec((None, 1, tn), lambda l, j: (l, 0, j))],
        out_specs=pl.BlockSpec((None, 8, tn), lambda l, j: (l, 0, j)),
        out_shape=jax.ShapeDtypeStruct((depth, 8, n), F32),
        compiler_params=_params(("arbitrary", "arbitrary")),
        name="modulation",
    )(c8, mod_w, mod_b.reshape(depth, 1, n))
    return out[:, 0, :]


def _proj_kernel(x_ref, nw_ref, sc_ref, sh_ref, w_ref, wg_ref, o_ref, g_ref):
    h = _norm_mod(x_ref[...], nw_ref[...], sc_ref[...], sh_ref[...])
    hb = h.astype(BF16)
    o_ref[...] = jnp.dot(hb, w_ref[...], preferred_element_type=F32)
    ng = g_ref.shape[-1]

    @pl.when(pl.program_id(0) == 0)
    def _():
        r = jnp.dot(hb, wg_ref[...], preferred_element_type=F32)
        g_ref[...] = r[:, :ng] + r[:, ng:]

    @pl.when(pl.program_id(0) != 0)
    def _():
        g_ref[...] = jnp.zeros_like(g_ref)


CAST_TN = 512


def _cast_kernel(g_blk, wt_ref, o_ref, g_ref):
    w = wt_ref[...].T
    o_ref[...] = w.astype(o_ref.dtype)

    @pl.when(pl.program_id(0) == g_blk)
    def _():
        g_ref[...] = w[:, :LANES]


def _cast_weight(w3, g0):
    _, d, n = w3.shape
    wt = jnp.swapaxes(w3, 1, 2).reshape(n, d)
    nblk = pl.cdiv(n, CAST_TN)
    return pl.pallas_call(
        functools.partial(_cast_kernel, g0 // CAST_TN),
        grid=(nblk,),
        in_specs=[pl.BlockSpec((CAST_TN, d), lambda j: (j, 0))],
        out_specs=[pl.BlockSpec((d, CAST_TN), lambda j: (0, j)), pl.BlockSpec((d, LANES), lambda j: (0, 0))],
        out_shape=[jax.ShapeDtypeStruct((d, nblk * CAST_TN), BF16), jax.ShapeDtypeStruct((d, LANES), F32)],
        compiler_params=_params(("arbitrary",)),
        name="cast_weight",
    )(wt)


def _split_gate_weights(wg):
    wg1 = wg.astype(BF16)
    wg2 = (wg - wg1.astype(F32)).astype(BF16)
    return jnp.concatenate([wg1, wg2], axis=1)


def _norm_proj(x, nw, sc, sh, w_bf16, n, wg, tm, n_split=2):
    t, d = x.shape
    ng = wg.shape[1]
    tn = n // n_split
    row = lambda s, i: (0, 0)
    out, gates = pl.pallas_call(
        _proj_kernel,
        grid=(n_split, t // tm),
        in_specs=[pl.BlockSpec((tm, d), lambda s, i: (i, 0)),
                  pl.BlockSpec((1, d), row), pl.BlockSpec((1, d), row), pl.BlockSpec((1, d), row),
                  pl.BlockSpec((d, tn), lambda s, i: (0, s), pipeline_mode=pl.Buffered(1)),
                  pl.BlockSpec((d, 2 * ng), row, pipeline_mode=pl.Buffered(1))],
        out_specs=[pl.BlockSpec((tm, tn), lambda s, i: (i, s)),
                   pl.BlockSpec((None, tm, ng), lambda s, i: (s, i, 0))],
        out_shape=[jax.ShapeDtypeStruct((t, n), F32), jax.ShapeDtypeStruct((n_split, t, ng), F32)],
        compiler_params=_params(("arbitrary", "arbitrary")),
        name="norm_proj",
    )(x, nw, sc, sh, w_bf16, _split_gate_weights(wg))
    return out, gates[0]


def _proj2_kernel(x_ref, nw_ref, sc_ref, sh_ref, wa_ref, wb_ref, wg_ref, o_ref, g_ref):
    h = _norm_mod(x_ref[...], nw_ref[...], sc_ref[...], sh_ref[...])
    hb = h.astype(BF16)
    na = wa_ref.shape[1]
    o_ref[:, :na] = jnp.dot(hb, wa_ref[...], preferred_element_type=F32)
    o_ref[:, na:] = jnp.dot(hb, wb_ref[...], preferred_element_type=F32)
    ng = g_ref.shape[-1]
    r = jnp.dot(hb, wg_ref[...], preferred_element_type=F32)
    g_ref[...] = r[:, :ng] + r[:, ng:]


def _norm_proj2(x, nw, sc, sh, wa_bf16, na, wb_bf16, wg, tm):
    t, d = x.shape
    nb = wb_bf16.shape[1]
    ng = wg.shape[1]
    row = lambda i: (0, 0)
    return pl.pallas_call(
        _proj2_kernel,
        grid=(t // tm,),
        in_specs=[pl.BlockSpec((tm, d), lambda i: (i, 0)),
                  pl.BlockSpec((1, d), row), pl.BlockSpec((1, d), row), pl.BlockSpec((1, d), row),
                  pl.BlockSpec((d, na), row, pipeline_mode=pl.Buffered(1)),
                  pl.BlockSpec((d, nb), row, pipeline_mode=pl.Buffered(1)),
                  pl.BlockSpec((d, 2 * ng), row, pipeline_mode=pl.Buffered(1))],
        out_specs=[pl.BlockSpec((tm, na + nb), lambda i: (i, 0)), pl.BlockSpec((tm, ng), lambda i: (i, 0))],
        out_shape=[jax.ShapeDtypeStruct((t, na + nb), F32), jax.ShapeDtypeStruct((t, ng), F32)],
        compiler_params=_params(("arbitrary",)),
        name="norm_proj2",
    )(x, nw, sc, sh, wa_bf16, wb_bf16, _split_gate_weights(wg))


def _out_kernel(n_in, *refs):
    y_refs = refs[:n_in]
    w_refs = refs[n_in:2 * n_in]
    x_ref, g_ref, nw_ref, sc_ref, sh_ref, o_ref, h_ref = refs[2 * n_in:]
    acc = jnp.dot(y_refs[0][...], w_refs[0][...], preferred_element_type=F32)
    for y_ref, w_ref in zip(y_refs[1:], w_refs[1:]):
        acc += jnp.dot(y_ref[...], w_ref[...], preferred_element_type=F32)
    x_new = x_ref[...] + g_ref[...] * acc
    o_ref[...] = x_new
    h_ref[...] = _norm_mod(x_new, nw_ref[...], sc_ref[...], sh_ref[...]).astype(h_ref.dtype)


def _out_proj(ys, ws, x, gate, nw, sc, sh, tm=512):
    t, d = x.shape
    n_in = len(ys)
    row = pl.BlockSpec((1, d), lambda i: (0, 0))
    in_specs = [pl.BlockSpec((tm, y.shape[1]), lambda i: (i, 0)) for y in ys]
    in_specs += [pl.BlockSpec(w.shape, lambda i: (0, 0), pipeline_mode=pl.Buffered(1)) for w in ws]
    in_specs += [pl.BlockSpec((tm, d), lambda i: (i, 0)), row, row, row, row]
    return pl.pallas_call(
        functools.partial(_out_kernel, n_in),
        grid=(t // tm,),
        in_specs=in_specs,
        out_specs=[pl.BlockSpec((tm, d), lambda i: (i, 0)), pl.BlockSpec((tm, d), lambda i: (i, 0))],
        out_shape=[jax.ShapeDtypeStruct((t, d), F32), jax.ShapeDtypeStruct((t, d), BF16)],
        compiler_params=_params(("arbitrary",)),
        name="out_proj",
    )(*ys, *[w.astype(BF16) for w in ws], x, gate, nw, sc, sh)


def _mlp_kernel(x_ref, h_ref, g_ref, w1_ref, w2_ref, o_ref):
    f = pl.program_id(1)

    @pl.when(f == 0)
    def _():
        o_ref[...] = jnp.zeros_like(o_ref)

    a = jnp.maximum(jnp.dot(h_ref[...], w1_ref[...], preferred_element_type=F32), 0.0)
    o_ref[...] += jnp.dot((a * a).astype(BF16), w2_ref[...], preferred_element_type=F32)

    @pl.when(f == pl.num_programs(1) - 1)
    def _():
        o_ref[...] = x_ref[...] + g_ref[...] * o_ref[...]


def _mlp(x, h, gate, w1_all, w2_all, layer, tm=1024, tf=512):
    t, d = x.shape
    ff = w1_all.shape[2]
    return pl.pallas_call(
        _mlp_kernel,
        grid=(t // tm, ff // tf),
        in_specs=[pl.BlockSpec((tm, d), lambda i, f: (i, 0)),
                  pl.BlockSpec((tm, d), lambda i, f: (i, 0)),
                  pl.BlockSpec((1, d), lambda i, f: (0, 0)),
                  pl.BlockSpec((None, d, tf), lambda i, f: (layer, 0, f)),
                  pl.BlockSpec((None, tf, d), lambda i, f: (layer, f, 0))],
        out_specs=pl.BlockSpec((tm, d), lambda i, f: (i, 0)),
        out_shape=jax.ShapeDtypeStruct((t, d), F32),
        compiler_params=_params(("parallel", "arbitrary")),
        name="mlp",
    )(x, h, gate, w1_all, w2_all)


ATT_TQ = 256
ATT_KB = 3
ATT_HG = 4


def _attn_kernel(q_ref, k_ref, v_ref, bias_ref, qw_ref, kw_ref, o_ref, kn_scr, v_scr):
    i = pl.program_id(0)

    @pl.when(i == 0)
    def _():
        kn_scr[...] = jnp.zeros_like(kn_scr)
        v_scr[...] = jnp.zeros_like(v_scr)

    def rms(x, w):
        return x * lax.rsqrt(jnp.mean(x * x, axis=-1, keepdims=True) + NORM_EPS) * w

    slots = [lax.rem(i + 1 + b, ATT_KB) for b in range(ATT_KB)]
    kw = kw_ref[...]
    qw = qw_ref[...]
    for h in range(ATT_HEADS):
        cols = slice(h * ATT_DH, (h + 1) * ATT_DH)
        kn_scr[slots[-1], :, cols] = rms(k_ref[:, cols], kw).astype(BF16)
    v_scr[slots[-1]] = v_ref[...].astype(BF16)

    for h0 in range(0, ATT_HEADS, ATT_HG):
        hs = range(h0, h0 + ATT_HG)
        col = {h: slice(h * ATT_DH, (h + 1) * ATT_DH) for h in hs}
        q = {h: (rms(q_ref[:, col[h]], qw) * (ATT_DH ** -0.5 * LOG2E)).astype(BF16) for h in hs}
        s = {}
        for h in hs:
            for b in range(ATT_KB):
                sb = lax.dot_general(q[h], kn_scr[slots[b], :, col[h]], _NT, preferred_element_type=F32)
                sb = sb + bias_ref[h, :, b * ATT_TQ:(b + 1) * ATT_TQ]
                s[h, b] = jnp.where(i >= ATT_KB - 1 - b, sb, NEG_BIG)
        p, l = {}, {}
        for h in hs:
            m = jnp.max(jnp.maximum(jnp.maximum(s[h, 0], s[h, 1]), s[h, 2]), axis=-1, keepdims=True)
            for b in range(ATT_KB):
                p[h, b] = jnp.exp2(s[h, b] - m)
            l[h] = jnp.sum(p[h, 0] + p[h, 1] + p[h, 2], axis=-1, keepdims=True)
        for h in hs:
            o = jnp.dot(p[h, 0].astype(BF16), v_scr[slots[0], :, col[h]], preferred_element_type=F32)
            for b in range(1, ATT_KB):
                o += jnp.dot(p[h, b].astype(BF16), v_scr[slots[b], :, col[h]], preferred_element_type=F32)
            o_ref[:, col[h]] = (o / l[h]).astype(o_ref.dtype)


def _attn_bias_table(rel_bias):
    nk = ATT_KB * ATT_TQ
    span = ATT_TQ + nk
    dist = jnp.arange(span) - (ATT_TQ - 1) - (ATT_KB - 1) * ATT_TQ
    ext = rel_bias.astype(F32)[:, jnp.clip(dist, -REL_CLIP, REL_CLIP) + REL_CLIP]
    flat = jnp.tile(ext, (1, ATT_TQ))[:, :ATT_TQ * (span - 1)]
    tab = flat.reshape(-1, ATT_TQ, span - 1)[:, :, ATT_TQ - 1:ATT_TQ - 1 + nk]
    r = jnp.arange(ATT_TQ)[:, None]
    m = jnp.arange(nk)[None, :]
    qc = r // CHUNK
    kc = m // CHUNK
    in_band = (kc >= qc) & (kc <= qc + BAND_CHUNKS - 1)
    return jnp.where(in_band[None], tab * LOG2E, NEG_BIG)


def _band_attention(proj, q_blk, k_blk, v_blk, q_norm_w, k_norm_w, rel_bias):
    t = proj.shape[0]
    bias = _attn_bias_table(rel_bias)
    blk = (ATT_TQ, ATT_W)
    return pl.pallas_call(
        _attn_kernel,
        grid=(t // ATT_TQ,),
        in_specs=[pl.BlockSpec(blk, lambda i: (i, q_blk)),
                  pl.BlockSpec(blk, lambda i: (i, k_blk)),
                  pl.BlockSpec(blk, lambda i: (i, v_blk)),
                  pl.BlockSpec(bias.shape, lambda i: (0, 0, 0), pipeline_mode=pl.Buffered(1)),
                  pl.BlockSpec((1, ATT_DH), lambda i: (0, 0)),
                  pl.BlockSpec((1, ATT_DH), lambda i: (0, 0))],
        out_specs=pl.BlockSpec(blk, lambda i: (i, 0)),
        out_shape=jax.ShapeDtypeStruct((t, ATT_W), BF16),
        scratch_shapes=[pltpu.VMEM((ATT_KB, ATT_TQ, ATT_W), BF16), pltpu.VMEM((ATT_KB, ATT_TQ, ATT_W), BF16)],
        compiler_params=_params(("arbitrary",)),
        name="band_attention",
    )(proj, proj, proj, bias, q_norm_w.reshape(1, ATT_DH), k_norm_w.reshape(1, ATT_DH))


GDN_TB = 1024
GDN_NC = GDN_TB // CHUNK
GDN_HPS = 4
GDN_GRP = 4


def _causal_conv(cur_ref, prev_ref, w_ref, scr, first_block):
    rows = cur_ref.shape[0]
    prev = prev_ref[...]
    scr[0:HALO, :] = jnp.where(first_block, jnp.zeros_like(prev), prev)
    scr[HALO:HALO + rows, :] = cur_ref[...]
    w = w_ref[...]
    ext = scr[...]
    acc = ext[HALO:, :] * w[CONV_K - 1:CONV_K, :]
    for s in range(1, CONV_K):
        acc += pltpu.roll(ext, s, 0)[HALO:, :] * w[CONV_K - 1 - s:CONV_K - s, :]
    return acc


def _block_diag(h):
    blk = lax.broadcasted_iota(jnp.int32, (1, h.shape[1]), 1) // CHUNK
    return jnp.concatenate([jnp.where(blk == j, h, jnp.zeros_like(h)) for j in range(h.shape[1] // CHUNK)], axis=0)


def _gdn_kernel(qc_ref, qp_ref, kc_ref, kp_ref, vc_ref, vp_ref, wq_ref, wk_ref, wv_ref,
                gate_ref, alog_ref, dtb_ref, z_ref, nw_ref, o_ref,
                cq_scr, ck_scr, cv_scr, beta_scr, gcum_scr, gt_scr,
                wq_scr, u_scr, qk_scr, a_scr, b_scr, s_scr):
    i = pl.program_id(0)
    hp = pl.program_id(1)
    tb = GDN_TB
    first = i == 0

    @pl.when(first)
    def _():
        for hl in range(GDN_HPS):
            s_scr[hp * GDN_HPS + hl] = jnp.zeros(s_scr.shape[1:], F32)

    @pl.when(hp == 0)
    def _():
        gate = gate_ref[...]
        beta_scr[...] = jax.nn.sigmoid(gate)
        g = -jnp.exp(alog_ref[...]) * _softplus(gate + dtb_ref[...]) * LOG2E
        gcum = _chunk_cumsum(g)
        gcum_scr[...] = gcum
        for s in range(tb // LANES):
            gt_scr[s] = gcum[s * LANES:(s + 1) * LANES, :].T

    lane = lax.broadcasted_iota(jnp.int32, (1, LANES), 1)
    r = lax.broadcasted_iota(jnp.int32, (CHUNK, CHUNK), 0)
    c = lax.broadcasted_iota(jnp.int32, (CHUNK, CHUNK), 1)
    causal = r >= c
    strict = r > c
    rh = lax.broadcasted_iota(jnp.int32, (CHUNK, GDN_GRP * CHUNK), 0)
    ch = lax.broadcasted_iota(jnp.int32, (CHUNK, GDN_GRP * CHUNK), 1)
    eye_h = jnp.where(rh == ch % CHUNK, 1.0, 0.0)
    nw = nw_ref[...]
    heads = range(GDN_HPS)
    groups = range(GDN_NC // GDN_GRP)
    gw = GDN_GRP * CHUNK

    def l2n(x):
        return x * lax.rsqrt(jnp.sum(x * x, axis=-1, keepdims=True) + NORM_EPS)

    qa = _silu(_causal_conv(qc_ref, qp_ref, wq_ref, cq_scr, first))
    ka = _silu(_causal_conv(kc_ref, kp_ref, wk_ref, ck_scr, first))
    va = _silu(_causal_conv(vc_ref, vp_ref, wv_ref, cv_scr, first))

    gcs, ks, rhs, low_hs, p_hs, x_hs = {}, {}, {}, {}, {}, {}
    for hl in heads:
        h = hp * GDN_HPS + hl
        cols = slice(hl * LANES, (hl + 1) * LANES)
        beta = jnp.sum(jnp.where(lane == h, beta_scr[...], 0.0), axis=1, keepdims=True)
        gc = jnp.sum(jnp.where(lane == GDN_HEADS + h, gcum_scr[...], 0.0), axis=1, keepdims=True)
        q = l2n(qa[:, cols]) * (GDN_DK ** -0.5)
        k = l2n(ka[:, cols])
        eg = jnp.exp2(gc)
        kb = k * beta
        rhs[hl] = jnp.concatenate([va[:, cols] * beta, kb * eg], axis=1).astype(BF16)
        qd = (q * eg).astype(BF16)
        gcs[hl], ks[hl] = gc, k
        for grp in groups:
            lows = []
            for j in range(GDN_GRP):
                n = grp * GDN_GRP + j
                rows = slice(n * CHUNK, (n + 1) * CHUNK)
                gc_n = gc[rows]
                grow = gt_scr[n // 2, pl.ds(GDN_HEADS + h, 1), :]
                grow = grow[:, (n % 2) * CHUNK:(n % 2 + 1) * CHUNK]
                decay = jnp.where(causal, jnp.exp2(jnp.where(causal, gc_n - grow, 0.0)), 0.0)
                ab = _dot_nt(jnp.concatenate([kb[rows], q[rows]], axis=0), k[rows])
                lows.append(jnp.where(strict, ab[:CHUNK] * decay, 0.0))
                qk_scr[hl, rows, :] = (ab[CHUNK:] * decay).astype(BF16)
                wq_scr[hl, n, CHUNK:2 * CHUNK, :] = qd[rows]
            low_hs[hl, grp] = jnp.concatenate(lows, axis=1)

    for key, low_h in low_hs.items():
        xb = low_h.astype(BF16)
        p_hs[key] = eye_h - low_h
        x_hs[key] = jnp.dot(xb, _block_diag(xb), preferred_element_type=F32)
    for step in range(5):
        for key in low_hs:
            xb = x_hs[key].astype(BF16)
            x_bd = _block_diag(xb)
            p_hs[key] = p_hs[key] + jnp.dot(p_hs[key].astype(BF16), x_bd, preferred_element_type=F32)
            if step < 4:
                x_hs[key] = jnp.dot(xb, x_bd, preferred_element_type=F32)

    for (hl, grp), p_h in p_hs.items():
        g0 = grp * gw
        uw = jnp.dot(_block_diag(p_h.astype(BF16)), rhs[hl][g0:g0 + gw], preferred_element_type=F32)
        u_scr[hl, g0:g0 + gw, :] = uw[:, :GDN_DV]
        for j in range(GDN_GRP):
            n = grp * GDN_GRP + j
            rows = slice(n * CHUNK, (n + 1) * CHUNK)
            uw_n = uw[j * CHUNK:(j + 1) * CHUNK]
            wq_scr[hl, n, 0:CHUNK, :] = uw_n[:, GDN_DV:].astype(BF16)
            gc_n = gcs[hl][rows]
            kd = ks[hl][rows] * jnp.exp2(gc_n[CHUNK - 1:CHUNK, :] - gc_n)
            ba = _dot_tn(kd, uw_n)
            b_scr[hl, n] = ba[:, :GDN_DV]
            a_scr[hl, n] = ba[:, GDN_DV:].astype(BF16)

    states = [s_scr[hp * GDN_HPS + hl] for hl in heads]
    for n in range(GDN_NC):
        rows = slice(n * CHUNK, (n + 1) * CHUNK)
        for hl in heads:
            cols = slice(hl * LANES, (hl + 1) * LANES)
            sb = states[hl].astype(BF16)
            ws = jnp.dot(wq_scr[hl, n], sb, preferred_element_type=F32)
            g_last = jnp.exp2(gcs[hl][(n + 1) * CHUNK - 1:(n + 1) * CHUNK, :])
            states[hl] = states[hl] * g_last - jnp.dot(a_scr[hl, n], sb, preferred_element_type=F32) + b_scr[hl, n]
            v_new = u_scr[hl, rows, :] - ws[:CHUNK]
            o = ws[CHUNK:] + jnp.dot(qk_scr[hl, rows, :], v_new.astype(BF16), preferred_element_type=F32)
            on = o * lax.rsqrt(jnp.mean(o * o, axis=-1, keepdims=True) + NORM_EPS) * nw
            o_ref[rows, cols] = (on * _silu(z_ref[rows, cols])).astype(o_ref.dtype)
    for hl in heads:
        s_scr[hp * GDN_HPS + hl] = states[hl]


def _gated_deltanet(proj, gates, conv_w, a_log, dt_bias, norm_w):
    t = proj.shape[0]
    tb = GDN_TB
    nh = GDN_HEADS
    w = GDN_HPS * LANES
    nb = GDN_QK // w
    pad = lambda a: jnp.zeros((1, LANES), F32).at[0, nh:2 * nh].set(a.astype(F32))
    cur = lambda sec: pl.BlockSpec((tb, w), lambda i, h: (i, sec * nb + h))
    prev = lambda sec: pl.BlockSpec((HALO, w), lambda i, h: (jnp.maximum(i * (tb // HALO) - 1, 0), sec * nb + h))
    cw = lambda sec: pl.BlockSpec((CONV_K, w), lambda i, h: (0, sec * nb + h))
    row = pl.BlockSpec((1, LANES), lambda i, h: (0, 0))
    return pl.pallas_call(
        _gdn_kernel,
        grid=(t // tb, nh // GDN_HPS),
        in_specs=[cur(0), prev(0), cur(1), prev(1), cur(2), prev(2), cw(0), cw(1), cw(2),
                  pl.BlockSpec((tb, LANES), lambda i, h: (i, 0)), row, row, cur(3), row],
        out_specs=pl.BlockSpec((tb, w), lambda i, h: (i, h)),
        out_shape=jax.ShapeDtypeStruct((t, GDN_V), BF16),
        scratch_shapes=[pltpu.VMEM((tb + HALO, w), F32)] * 3 + [
            pltpu.VMEM((tb, LANES), F32), pltpu.VMEM((tb, LANES), F32),
            pltpu.VMEM((tb // LANES, LANES, LANES), F32),
            pltpu.VMEM((GDN_HPS, GDN_NC, 2 * CHUNK, GDN_DK), BF16),
            pltpu.VMEM((GDN_HPS, tb, GDN_DV), F32),
            pltpu.VMEM((GDN_HPS, tb, CHUNK), BF16),
            pltpu.VMEM((GDN_HPS, GDN_NC, GDN_DK, GDN_DK), BF16),
            pltpu.VMEM((GDN_HPS, GDN_NC, GDN_DK, GDN_DV), F32),
            pltpu.VMEM((nh, GDN_DK, GDN_DV), F32)],
        compiler_params=_params(("arbitrary", "arbitrary")),
        name="gated_deltanet",
    )(proj, proj, proj, proj, proj, proj, conv_w, conv_w, conv_w,
      gates, pad(a_log), pad(dt_bias), proj, norm_w.reshape(1, GDN_DV))


SSD_TB = 512
SSD_NC = SSD_TB // CHUNK


def _ssd_expand_matrix():
    j = jnp.arange(LANES)[:, None]
    col = jnp.arange(2 * SSD_GW)[None, :]
    head = (col % SSD_GW) // SSD_HEADDIM
    piece = j // SSD_HPG
    is_ac = (col < SSD_GW) & (piece < 3)
    is_dt = (col >= SSD_GW) & (piece >= 3) & (piece < 5)
    return jnp.where((is_ac | is_dt) & (j % SSD_HPG == head), 1.0, 0.0).astype(BF16)


def _ssd_kernel(xc_ref, xp_ref, bc_ref, bp_ref, cc_ref, cp_ref, wx_ref, wb_ref, wc_ref,
                bx_ref, bb_ref, bcn_ref, dt_ref, dtb_ref, alog_ref, dsk_ref, z_ref, nw_ref, ex_ref, o_ref,
                cx_scr, cb_scr, cc_scr, dt_scr, ac_scr, x_scr, e_scr, st_scr):
    i = pl.program_id(0)
    g = pl.program_id(1)
    tb = SSD_TB
    hpg = SSD_HPG
    first = i == 0

    @pl.when(first)
    def _():
        st_scr[g] = jnp.zeros(st_scr.shape[1:], F32)

    @pl.when(g == 0)
    def _():
        dt = _softplus(dt_ref[...] + dtb_ref[...])
        dt_scr[...] = dt
        ac_scr[...] = _chunk_cumsum(dt * (-jnp.exp(alog_ref[...]) * LOG2E))

    shift = (LANES - g * hpg) % LANES
    lane = lax.broadcasted_iota(jnp.int32, (1, LANES), 1)
    mine = lane < hpg
    dt_g = jnp.where(mine, pltpu.roll(dt_scr[...], shift, 1), 0.0)
    ac_g = jnp.where(mine, pltpu.roll(ac_scr[...], shift, 1), 0.0)

    a1, a2, a3 = _split3(ac_g)
    d1 = dt_g.astype(BF16)
    d2 = (dt_g - d1.astype(F32)).astype(BF16)
    packed = a1.astype(F32)
    for k, piece in enumerate((a2, a3, d1, d2), start=1):
        packed = packed + pltpu.roll(piece.astype(F32), k * hpg, 1)
    e_scr[...] = jnp.dot(packed.astype(BF16), ex_ref[...], preferred_element_type=F32)

    ac_t = [ac_g[s * LANES:(s + 1) * LANES, :].T[0:8, :] for s in range(tb // LANES)]
    ac_tr = [pltpu.roll(a, CHUNK, 1) for a in ac_t]

    x_scr[...] = _silu(_causal_conv(xc_ref, xp_ref, wx_ref, cx_scr, first) + bx_ref[...])
    bm = _silu(_causal_conv(bc_ref, bp_ref, wb_ref, cb_scr, first) + bb_ref[...])
    cm = _silu(_causal_conv(cc_ref, cp_ref, wc_ref, cc_scr, first) + bcn_ref[...])

    r = lax.broadcasted_iota(jnp.int32, (CHUNK, LANES), 0)
    c = lax.broadcasted_iota(jnp.int32, (CHUNK, LANES), 1)
    causal2 = r >= (c % CHUNK)
    left = c < CHUNK
    dsk = dsk_ref[...]
    nw = nw_ref[...]
    gw = SSD_GW

    chunks = range(SSD_NC)
    rows = [slice(n * CHUNK, (n + 1) * CHUNK) for n in chunks]
    cbs = [_dot_nt(cm[rows[n]], bm[rows[n]]) for n in chunks]
    y_diag, upd, dec = [], [], []
    for n in chunks:
        acx = e_scr[rows[n], 0:gw]
        xdt = x_scr[rows[n], :] * e_scr[rows[n], gw:2 * gw]
        a_last = acx[CHUNK - 1:CHUNK, :]
        cb2 = jnp.concatenate([cbs[n], cbs[n]], axis=1)
        t_lo, t_hi = (ac_t[n // 2], ac_tr[n // 2]) if n % 2 == 0 else (ac_tr[n // 2], ac_t[n // 2])
        ys = []
        for p in range(hpg // 2):
            lanes = slice(p * LANES, (p + 1) * LANES)
            a_row = jnp.where(lane < CHUNK, t_lo[2 * p:2 * p + 1, :], t_hi[2 * p + 1:2 * p + 2, :])
            decay = jnp.where(causal2, jnp.exp2(jnp.where(causal2, acx[:, lanes] - a_row, 0.0)), 0.0)
            xp = xdt[:, lanes]
            x_bd = jnp.concatenate([jnp.where(left, xp, 0.0), jnp.where(left, 0.0, xp)], axis=0)
            ys.append(_dot(cb2 * decay, x_bd))
        y_diag.append(jnp.concatenate(ys, axis=1))
        upd.append(_dot_tn(bm[rows[n]], xdt * jnp.exp2(a_last - acx)))
        dec.append(jnp.exp2(a_last))

    state = st_scr[g]
    states = []
    for n in chunks:
        states.append(state.astype(BF16))
        state = state * dec[n] + upd[n]
    st_scr[g] = state

    for n in chunks:
        acx = e_scr[rows[n], 0:gw]
        y_off = jnp.dot(cm[rows[n]].astype(BF16), states[n], preferred_element_type=F32)
        y = y_diag[n] + y_off * jnp.exp2(acx) + x_scr[rows[n], :] * dsk
        y = y * _silu(z_ref[rows[n], :])
        y = y * lax.rsqrt(jnp.mean(y * y, axis=-1, keepdims=True) + NORM_EPS) * nw
        o_ref[rows[n], :] = y.astype(o_ref.dtype)


def _mamba2_ssd(proj, dt_raw, conv_w, conv_b, dt_bias, a_log, d_skip, norm_w):
    t = proj.shape[0]
    tb = SSD_TB
    gw = SSD_GW
    ng = SSD_GROUPS
    x0 = SSD_DINNER // gw
    b0 = 2 * SSD_DINNER // LANES
    c0 = b0 + ng
    pad = lambda a: jnp.zeros((1, LANES), F32).at[0, :SSD_HEADS].set(a.astype(F32))
    prev_idx = lambda i: jnp.maximum(i * (tb // HALO) - 1, 0)
    cbias = conv_b.reshape(1, -1)
    dsk = jnp.repeat(d_skip.astype(F32), SSD_HEADDIM).reshape(1, SSD_DINNER)
    return pl.pallas_call(
        _ssd_kernel,
        grid=(t // tb, ng),
        in_specs=[pl.BlockSpec((tb, gw), lambda i, g: (i, x0 + g)),
                  pl.BlockSpec((HALO, gw), lambda i, g: (prev_idx(i), x0 + g)),
                  pl.BlockSpec((tb, LANES), lambda i, g: (i, b0 + g)),
                  pl.BlockSpec((HALO, LANES), lambda i, g: (prev_idx(i), b0 + g)),
                  pl.BlockSpec((tb, LANES), lambda i, g: (i, c0 + g)),
                  pl.BlockSpec((HALO, LANES), lambda i, g: (prev_idx(i), c0 + g)),
                  pl.BlockSpec((CONV_K, gw), lambda i, g: (0, g)),
                  pl.BlockSpec((CONV_K, LANES), lambda i, g: (0, SSD_DINNER // LANES + g)),
                  pl.BlockSpec((CONV_K, LANES), lambda i, g: (0, SSD_DINNER // LANES + ng + g)),
                  pl.BlockSpec((1, gw), lambda i, g: (0, g)),
                  pl.BlockSpec((1, LANES), lambda i, g: (0, SSD_DINNER // LANES + g)),
                  pl.BlockSpec((1, LANES), lambda i, g: (0, SSD_DINNER // LANES + ng + g)),
                  pl.BlockSpec((tb, LANES), lambda i, g: (i, 0)),
                  pl.BlockSpec((1, LANES), lambda i, g: (0, 0)),
                  pl.BlockSpec((1, LANES), lambda i, g: (0, 0)),
                  pl.BlockSpec((1, gw), lambda i, g: (0, g)),
                  pl.BlockSpec((tb, gw), lambda i, g: (i, g)),
                  pl.BlockSpec((1, gw), lambda i, g: (0, g)),
                  pl.BlockSpec((LANES, 2 * gw), lambda i, g: (0, 0))],
        out_specs=pl.BlockSpec((tb, gw), lambda i, g: (i, g)),
        out_shape=jax.ShapeDtypeStruct((t, SSD_DINNER), BF16),
        scratch_shapes=[pltpu.VMEM((tb + HALO, gw), F32), pltpu.VMEM((tb + HALO, LANES), F32),
                        pltpu.VMEM((tb + HALO, LANES), F32),
                        pltpu.VMEM((tb, LANES), F32), pltpu.VMEM((tb, LANES), F32),
                        pltpu.VMEM((tb, gw), F32), pltpu.VMEM((tb, 2 * gw), F32),
                        pltpu.VMEM((ng, SSD_DSTATE, gw), F32)],
        compiler_params=_params(("arbitrary", "arbitrary")),
        name="mamba2_ssd",
    )(proj, proj, proj, proj, proj, proj, conv_w, conv_w, conv_w, cbias, cbias, cbias,
      dt_raw, pad(dt_bias), pad(a_log), dsk, proj, norm_w.reshape(1, SSD_DINNER), _ssd_expand_matrix())


def _layer_mods(mod, layer):
    d = D_MODEL
    return [mod[layer, k * d:(k + 1) * d].reshape(1, d) for k in range(6)]


def kernel(x, c, mod_w, mod_b, norm_mix_w, norm_mlp_w, mlp_w1, mlp_w2, ab_w_in, gdn_conv_w, gdn_a_log,
           gdn_dt_bias, gdn_norm_w, attn_q_norm_w, attn_k_norm_w, attn_rel_bias, ab_w_out, ssd_w_in,
           ssd_conv_w, ssd_conv_b, ssd_dt_bias, ssd_a_log, ssd_d, ssd_norm_w, ssd_w_out):
    b, t, d = x.shape
    assert b == 1 and d == D_MODEL
    xs = x.reshape(t, d)
    mod = _modulation(c, mod_w, mod_b)
    w1_b, w2_b = mlp_w1.astype(BF16), mlp_w2.astype(BF16)

    sh1, sc1, g1, sh2, sc2, g2 = _layer_mods(mod, 0)
    gate_lo = 2 * GDN_QK + 2 * GDN_V
    gate_hi = gate_lo + 2 * GDN_HEADS
    w_in_b, w_gate = _cast_weight(ab_w_in, gate_lo)
    w_gate = jnp.pad(w_gate[:, :2 * GDN_HEADS], ((0, 0), (0, LANES - 2 * GDN_HEADS)))
    proj, gates = _norm_proj2(xs, norm_mix_w[0].reshape(1, d), sc1, sh1, w_in_b, gate_lo, w_in_b[:, gate_hi:ab_w_in.shape[2]],
                              w_gate, tm=256)
    o_a = _gated_deltanet(proj, gates, gdn_conv_w[0], gdn_a_log[0], gdn_dt_bias[0], gdn_norm_w[0])
    a0 = gate_lo // ATT_W
    o_b = _band_attention(proj, a0, a0 + 1, a0 + 2, attn_q_norm_w[0], attn_k_norm_w[0], attn_rel_bias[0])
    xs, h2 = _out_proj([o_a, o_b], [ab_w_out[0][:GDN_V], ab_w_out[0][GDN_V:]], xs, g1,
                       norm_mlp_w[0].reshape(1, d), sc2, sh2)
    xs = _mlp(xs, h2, g2, w1_b, w2_b, 0)

    sh1, sc1, g1, sh2, sc2, g2 = _layer_mods(mod, 1)
    n_main = 2 * SSD_DINNER + 2 * SSD_GROUPS * SSD_DSTATE
    w_in_b, w_dt = _cast_weight(ssd_w_in, n_main)
    w_dt = jnp.pad(w_dt[:, :SSD_HEADS], ((0, 0), (0, LANES - SSD_HEADS)))
    proj, dt_raw = _norm_proj(xs, norm_mix_w[1].reshape(1, d), sc1, sh1, w_in_b, n_main, w_dt, tm=256)
    y = _mamba2_ssd(proj, dt_raw, ssd_conv_w[0], ssd_conv_b[0], ssd_dt_bias[0], ssd_a_log[0],
                    ssd_d[0], ssd_norm_w[0])
    xs, h2 = _out_proj([y], [ssd_w_out[0]], xs, g1, norm_mlp_w[1].reshape(1, d), sc2, sh2)
    xs = _mlp(xs, h2, g2, w1_b, w2_b, 1)
    return xs.reshape(b, t, d)
```

```python
import functools

import jax
import jax.numpy as jnp
from jax import lax
from jax.experimental import pallas as pl
from jax.experimental.pallas import tpu as pltpu

F32 = jnp.float32
BF16 = jnp.bfloat16

D_MODEL = 2048
CHUNK = 64
NORM_EPS = 1e-6
CONV_K = 4
HALO = 8
GDN_HEADS = 8
GDN_DK = 128
GDN_DV = 128
GDN_QK = GDN_HEADS * GDN_DK
GDN_V = GDN_HEADS * GDN_DV
ATT_HEADS = 8
ATT_DH = 128
ATT_W = ATT_HEADS * ATT_DH
BAND_CHUNKS = 9
REL_CLIP = 256
SSD_DINNER = 2 * D_MODEL
SSD_HEADDIM = 64
SSD_HEADS = SSD_DINNER // SSD_HEADDIM
SSD_GROUPS = 8
SSD_HPG = SSD_HEADS // SSD_GROUPS
SSD_DSTATE = 128
SSD_GW = SSD_HPG * SSD_HEADDIM
D_FF = 4 * D_MODEL
LANES = 128
NEG_BIG = -1e30
LOG2E = 1.4426950408889634

VMEM_LIMIT = 56 * 1024 * 1024

_NT = (((1,), (1,)), ((), ()))
_TN = (((0,), (0,)), ((), ()))


def _dot(a, b):
    return jnp.dot(a.astype(BF16), b.astype(BF16), preferred_element_type=F32)


def _dot_nt(a, b):
    return lax.dot_general(a.astype(BF16), b.astype(BF16), _NT, preferred_element_type=F32)


def _dot_tn(a, b):
    return lax.dot_general(a.astype(BF16), b.astype(BF16), _TN, preferred_element_type=F32)


def _split3(a):
    a1 = a.astype(BF16)
    r1 = a - a1.astype(F32)
    a2 = r1.astype(BF16)
    a3 = (r1 - a2.astype(F32)).astype(BF16)
    return a1, a2, a3


def _dot_exact_lhs(a_bf16, b):
    b1, b2, b3 = _split3(b)
    out = jnp.dot(a_bf16, b1, preferred_element_type=F32)
    out += jnp.dot(a_bf16, b2, preferred_element_type=F32)
    out += jnp.dot(a_bf16, b3, preferred_element_type=F32)
    return out


def _silu(x):
    return x * jax.nn.sigmoid(x)


def _softplus(x):
    return jnp.maximum(x, 0.0) + jnp.log(1.0 + jnp.exp(-jnp.abs(x)))


def _norm_mod(x, nw, sc, sh):
    ms = jnp.mean(x * x, axis=-1, keepdims=True)
    return (x * lax.rsqrt(ms + NORM_EPS) * nw) * (1.0 + sc) + sh


def _seg_tri(n):
    r = lax.broadcasted_iota(jnp.int32, (n, n), 0)
    c = lax.broadcasted_iota(jnp.int32, (n, n), 1)
    return jnp.where((r >= c) & ((r // CHUNK) == (c // CHUNK)), 1.0, 0.0).astype(BF16)


MXU_K = 256


def _chunk_cumsum(g):
    tri = _seg_tri(MXU_K)
    return jnp.concatenate([_dot_exact_lhs(tri, g[s:s + MXU_K]) for s in range(0, g.shape[0], MXU_K)], axis=0)


def _params(sem):
    return pltpu.CompilerParams(dimension_semantics=sem, vmem_limit_bytes=VMEM_LIMIT)


def _mod_kernel(c_ref, w_ref, b_ref, o_ref):
    c = c_ref[...]
    o_ref[...] = jnp.dot(_silu(c), w_ref[...], preferred_element_type=F32,
                         precision=lax.Precision.HIGHEST) + b_ref[...]


def _modulation(c, mod_w, mod_b, tn=1024):
    depth, d, n = mod_w.shape
    c8 = jnp.broadcast_to(c.reshape(1, d), (8, d))
    out = pl.pallas_call(
        _mod_kernel,
        grid=(depth, n // tn),
        in_specs=[pl.BlockSpec((8, d), lambda l, j: (0, 0)),
                  pl.BlockSpec((None, d, tn), lambda l, j: (l, 0, j)),
                  pl.BlockSpec((None, 1, tn), lambda l, j: (l, 0, j))],
        out_specs=pl.BlockSpec((None, 8, tn), lambda l, j: (l, 0, j)),
        out_shape=jax.ShapeDtypeStruct((depth, 8, n), F32),
        compiler_params=_params(("arbitrary", "arbitrary")),
        name="modulation",
    )(c8, mod_w, mod_b.reshape(depth, 1, n))
    return out[:, 0, :]


def _proj_kernel(x_ref, nw_ref, sc_ref, sh_ref, w_ref, wg_ref, o_ref, g_ref):
    h = _norm_mod(x_ref[...], nw_ref[...], sc_ref[...], sh_ref[...])
    hb = h.astype(BF16)
    o_ref[...] = jnp.dot(hb, w_ref[...], preferred_element_type=F32)
    ng = g_ref.shape[-1]

    @pl.when(pl.program_id(0) == 0)
    def _():
        r = jnp.dot(hb, wg_ref[...], preferred_element_type=F32)
        g_ref[...] = r[:, :ng] + r[:, ng:]

    @pl.when(pl.program_id(0) != 0)
    def _():
        g_ref[...] = jnp.zeros_like(g_ref)


CAST_TN = 512


def _cast_kernel(g_blk, wt_ref, o_ref, g_ref):
    w = wt_ref[...].T
    o_ref[...] = w.astype(o_ref.dtype)

    @pl.when(pl.program_id(0) == g_blk)
    def _():
        g_ref[...] = w[:, :LANES]


def _cast_weight(w3, g0):
    _, d, n = w3.shape
    wt = jnp.swapaxes(w3, 1, 2).reshape(n, d)
    nblk = pl.cdiv(n, CAST_TN)
    return pl.pallas_call(
        functools.partial(_cast_kernel, g0 // CAST_TN),
        grid=(nblk,),
        in_specs=[pl.BlockSpec((CAST_TN, d), lambda j: (j, 0))],
        out_specs=[pl.BlockSpec((d, CAST_TN), lambda j: (0, j)), pl.BlockSpec((d, LANES), lambda j: (0, 0))],
        out_shape=[jax.ShapeDtypeStruct((d, nblk * CAST_TN), BF16), jax.ShapeDtypeStruct((d, LANES), F32)],
        compiler_params=_params(("arbitrary",)),
        name="cast_weight",
    )(wt)


def _split_gate_weights(wg):
    wg1 = wg.astype(BF16)
    wg2 = (wg - wg1.astype(F32)).astype(BF16)
    return jnp.concatenate([wg1, wg2], axis=1)


def _norm_proj(x, nw, sc, sh, w_bf16, n, wg, tm, n_split=2):
    t, d = x.shape
    ng = wg.shape[1]
    tn = n // n_split
    row = lambda s, i: (0, 0)
    out, gates = pl.pallas_call(
        _proj_kernel,
        grid=(n_split, t // tm),
        in_specs=[pl.BlockSpec((tm, d), lambda s, i: (i, 0)),
                  pl.BlockSpec((1, d), row), pl.BlockSpec((1, d), row), pl.BlockSpec((1, d), row),
                  pl.BlockSpec((d, tn), lambda s, i: (0, s), pipeline_mode=pl.Buffered(1)),
                  pl.BlockSpec((d, 2 * ng), row, pipeline_mode=pl.Buffered(1))],
        out_specs=[pl.BlockSpec((tm, tn), lambda s, i: (i, s)),
                   pl.BlockSpec((None, tm, ng), lambda s, i: (s, i, 0))],
        out_shape=[jax.ShapeDtypeStruct((t, n), F32), jax.ShapeDtypeStruct((n_split, t, ng), F32)],
        compiler_params=_params(("arbitrary", "arbitrary")),
        name="norm_proj",
    )(x, nw, sc, sh, w_bf16, _split_gate_weights(wg))
    return out, gates[0]


def _proj2_kernel(x_ref, nw_ref, sc_ref, sh_ref, wa_ref, wb_ref, wg_ref, o_ref, g_ref):
    h = _norm_mod(x_ref[...], nw_ref[...], sc_ref[...], sh_ref[...])
    hb = h.astype(BF16)
    na = wa_ref.shape[1]
    o_ref[:, :na] = jnp.dot(hb, wa_ref[...], preferred_element_type=F32)
    o_ref[:, na:] = jnp.dot(hb, wb_ref[...], preferred_element_type=F32)
    ng = g_ref.shape[-1]
    r = jnp.dot(hb, wg_ref[...], preferred_element_type=F32)
    g_ref[...] = r[:, :ng] + r[:, ng:]


def _norm_proj2(x, nw, sc, sh, wa_bf16, na, wb_bf16, wg, tm):
    t, d = x.shape
    nb = wb_bf16.shape[1]
    ng = wg.shape[1]
    row = lambda i: (0, 0)
    return pl.pallas_call(
        _proj2_kernel,
        grid=(t // tm,),
        in_specs=[pl.BlockSpec((tm, d), lambda i: (i, 0)),
                  pl.BlockSpec((1, d), row), pl.BlockSpec((1, d), row), pl.BlockSpec((1, d), row),
                  pl.BlockSpec((d, na), row, pipeline_mode=pl.Buffered(1)),
                  pl.BlockSpec((d, nb), row, pipeline_mode=pl.Buffered(1)),
                  pl.BlockSpec((d, 2 * ng), row, pipeline_mode=pl.Buffered(1))],
        out_specs=[pl.BlockSpec((tm, na + nb), lambda i: (i, 0)), pl.BlockSpec((tm, ng), lambda i: (i, 0))],
        out_shape=[jax.ShapeDtypeStruct((t, na + nb), F32), jax.ShapeDtypeStruct((t, ng), F32)],
        compiler_params=_params(("arbitrary",)),
        name="norm_proj2",
    )(x, nw, sc, sh, wa_bf16, wb_bf16, _split_gate_weights(wg))


def _out_kernel(n_in, *refs):
    y_refs = refs[:n_in]
    w_refs = refs[n_in:2 * n_in]
    x_ref, g_ref, nw_ref, sc_ref, sh_ref, o_ref, h_ref = refs[2 * n_in:]
    acc = jnp.dot(y_refs[0][...], w_refs[0][...], preferred_element_type=F32)
    for y_ref, w_ref in zip(y_refs[1:], w_refs[1:]):
        acc += jnp.dot(y_ref[...], w_ref[...], preferred_element_type=F32)
    x_new = x_ref[...] + g_ref[...] * acc
    o_ref[...] = x_new
    h_ref[...] = _norm_mod(x_new, nw_ref[...], sc_ref[...], sh_ref[...]).astype(h_ref.dtype)


def _out_proj(ys, ws, x, gate, nw, sc, sh, tm=512):
    t, d = x.shape
    n_in = len(ys)
    row = pl.BlockSpec((1, d), lambda i: (0, 0))
    in_specs = [pl.BlockSpec((tm, y.shape[1]), lambda i: (i, 0)) for y in ys]
    in_specs += [pl.BlockSpec(w.shape, lambda i: (0, 0), pipeline_mode=pl.Buffered(1)) for w in ws]
    in_specs += [pl.BlockSpec((tm, d), lambda i: (i, 0)), row, row, row, row]
    return pl.pallas_call(
        functools.partial(_out_kernel, n_in),
        grid=(t // tm,),
        in_specs=in_specs,
        out_specs=[pl.BlockSpec((tm, d), lambda i: (i, 0)), pl.BlockSpec((tm, d), lambda i: (i, 0))],
        out_shape=[jax.ShapeDtypeStruct((t, d), F32), jax.ShapeDtypeStruct((t, d), BF16)],
        compiler_params=_params(("arbitrary",)),
        name="out_proj",
    )(*ys, *[w.astype(BF16) for w in ws], x, gate, nw, sc, sh)


def _mlp_kernel(x_ref, h_ref, g_ref, w1_ref, w2_ref, o_ref):
    f = pl.program_id(1)

    @pl.when(f == 0)
    def _():
        o_ref[...] = jnp.zeros_like(o_ref)

    a = jnp.maximum(jnp.dot(h_ref[...], w1_ref[...], preferred_element_type=F32), 0.0)
    o_ref[...] += jnp.dot((a * a).astype(BF16), w2_ref[...], preferred_element_type=F32)

    @pl.when(f == pl.num_programs(1) - 1)
    def _():
        o_ref[...] = x_ref[...] + g_ref[...] * o_ref[...]


def _mlp(x, h, gate, w1_all, w2_all, layer, tm=1024, tf=512):
    t, d = x.shape
    ff = w1_all.shape[2]
    return pl.pallas_call(
        _mlp_kernel,
        grid=(t // tm, ff // tf),
        in_specs=[pl.BlockSpec((tm, d), lambda i, f: (i, 0)),
                  pl.BlockSpec((tm, d), lambda i, f: (i, 0)),
                  pl.BlockSpec((1, d), lambda i, f: (0, 0)),
                  pl.BlockSpec((None, d, tf), lambda i, f: (layer, 0, f)),
                  pl.BlockSpec((None, tf, d), lambda i, f: (layer, f, 0))],
        out_specs=pl.BlockSpec((tm, d), lambda i, f: (i, 0)),
        out_shape=jax.ShapeDtypeStruct((t, d), F32),
        compiler_params=_params(("parallel", "arbitrary")),
        name="mlp",
    )(x, h, gate, w1_all, w2_all)


ATT_TQ = 256
ATT_KB = 3
ATT_HG = 4


def _attn_kernel(q_ref, k_ref, v_ref, bias_ref, qw_ref, kw_ref, o_ref, kn_scr, v_scr):
    i = pl.program_id(0)

    @pl.when(i == 0)
    def _():
        kn_scr[...] = jnp.zeros_like(kn_scr)
        v_scr[...] = jnp.zeros_like(v_scr)

    def rms(x, w):
        return x * lax.rsqrt(jnp.mean(x * x, axis=-1, keepdims=True) + NORM_EPS) * w

    slots = [lax.rem(i + 1 + b, ATT_KB) for b in range(ATT_KB)]
    kw = kw_ref[...]
    qw = qw_ref[...]
    for h in range(ATT_HEADS):
        cols = slice(h * ATT_DH, (h + 1) * ATT_DH)
        kn_scr[slots[-1], :, cols] = rms(k_ref[:, cols], kw).astype(BF16)
    v_scr[slots[-1]] = v_ref[...].astype(BF16)

    for h0 in range(0, ATT_HEADS, ATT_HG):
        hs = range(h0, h0 + ATT_HG)
        col = {h: slice(h * ATT_DH, (h + 1) * ATT_DH) for h in hs}
        q = {h: (rms(q_ref[:, col[h]], qw) * (ATT_DH ** -0.5 * LOG2E)).astype(BF16) for h in hs}
        s = {}
        for h in hs:
            for b in range(ATT_KB):
                sb = lax.dot_general(q[h], kn_scr[slots[b], :, col[h]], _NT, preferred_element_type=F32)
                sb = sb + bias_ref[h, :, b * ATT_TQ:(b + 1) * ATT_TQ]
                s[h, b] = jnp.where(i >= ATT_KB - 1 - b, sb, NEG_BIG)
        p, l = {}, {}
        for h in hs:
            m = jnp.max(jnp.maximum(jnp.maximum(s[h, 0], s[h, 1]), s[h, 2]), axis=-1, keepdims=True)
            for b in range(ATT_KB):
                p[h, b] = jnp.exp2(s[h, b] - m)
            l[h] = jnp.sum(p[h, 0] + p[h, 1] + p[h, 2], axis=-1, keepdims=True)
        for h in hs:
            o = jnp.dot(p[h, 0].astype(BF16), v_scr[slots[0], :, col[h]], preferred_element_type=F32)
            for b in range(1, ATT_KB):
                o += jnp.dot(p[h, b].astype(BF16), v_scr[slots[b], :, col[h]], preferred_element_type=F32)
            o_ref[:, col[h]] = (o / l[h]).astype(o_ref.dtype)


def _attn_bias_table(rel_bias):
    nk = ATT_KB * ATT_TQ
    span = ATT_TQ + nk
    dist = jnp.arange(span) - (ATT_TQ - 1) - (ATT_KB - 1) * ATT_TQ
    ext = rel_bias.astype(F32)[:, jnp.clip(dist, -REL_CLIP, REL_CLIP) + REL_CLIP]
    flat = jnp.tile(ext, (1, ATT_TQ))[:, :ATT_TQ * (span - 1)]
    tab = flat.reshape(-1, ATT_TQ, span - 1)[:, :, ATT_TQ - 1:ATT_TQ - 1 + nk]
    r = jnp.arange(ATT_TQ)[:, None]
    m = jnp.arange(nk)[None, :]
    qc = r // CHUNK
    kc = m // CHUNK
    in_band = (kc >= qc) & (kc <= qc + BAND_CHUNKS - 1)
    return jnp.where(in_band[None], tab * LOG2E, NEG_BIG)


def _band_attention(proj, q_blk, k_blk, v_blk, q_norm_w, k_norm_w, rel_bias):
    t = proj.shape[0]
    bias = _attn_bias_table(rel_bias)
    blk = (ATT_TQ, ATT_W)
    return pl.pallas_call(
        _attn_kernel,
        grid=(t // ATT_TQ,),
        in_specs=[pl.BlockSpec(blk, lambda i: (i, q_blk)),
                  pl.BlockSpec(blk, lambda i: (i, k_blk)),
                  pl.BlockSpec(blk, lambda i: (i, v_blk)),
                  pl.BlockSpec(bias.shape, lambda i: (0, 0, 0), pipeline_mode=pl.Buffered(1)),
                  pl.BlockSpec((1, ATT_DH), lambda i: (0, 0)),
                  pl.BlockSpec((1, ATT_DH), lambda i: (0, 0))],
        out_specs=pl.BlockSpec(blk, lambda i: (i, 0)),
        out_shape=jax.ShapeDtypeStruct((t, ATT_W), BF16),
        scratch_shapes=[pltpu.VMEM((ATT_KB, ATT_TQ, ATT_W), BF16), pltpu.VMEM((ATT_KB, ATT_TQ, ATT_W), BF16)],
        compiler_params=_params(("arbitrary",)),
        name="band_attention",
    )(proj, proj, proj, bias, q_norm_w.reshape(1, ATT_DH), k_norm_w.reshape(1, ATT_DH))


GDN_TB = 1024
GDN_NC = GDN_TB // CHUNK
GDN_HPS = 4
GDN_GRP = 4


def _causal_conv(cur_ref, prev_ref, w_ref, scr, first_block):
    rows = cur_ref.shape[0]
    prev = prev_ref[...]
    scr[0:HALO, :] = jnp.where(first_block, jnp.zeros_like(prev), prev)
    scr[HALO:HALO + rows, :] = cur_ref[...]
    w = w_ref[...]
    ext = scr[...]
    acc = ext[HALO:, :] * w[CONV_K - 1:CONV_K, :]
    for s in range(1, CONV_K):
        acc += pltpu.roll(ext, s, 0)[HALO:, :] * w[CONV_K - 1 - s:CONV_K - s, :]
    return acc


def _block_diag(h):
    blk = lax.broadcasted_iota(jnp.int32, (1, h.shape[1]), 1) // CHUNK
    return jnp.concatenate([jnp.where(blk == j, h, jnp.zeros_like(h)) for j in range(h.shape[1] // CHUNK)], axis=0)


def _gdn_kernel(qc_ref, qp_ref, kc_ref, kp_ref, vc_ref, vp_ref, wq_ref, wk_ref, wv_ref,
                gate_ref, alog_ref, dtb_ref, z_ref, nw_ref, o_ref,
                cq_scr, ck_scr, cv_scr, beta_scr, gcum_scr, gt_scr,
                wq_scr, u_scr, qk_scr, a_scr, b_scr, s_scr):
    i = pl.program_id(0)
    hp = pl.program_id(1)
    tb = GDN_TB
    first = i == 0

    @pl.when(first)
    def _():
        for hl in range(GDN_HPS):
            s_scr[hp * GDN_HPS + hl] = jnp.zeros(s_scr.shape[1:], F32)

    @pl.when(hp == 0)
    def _():
        gate = gate_ref[...]
        beta_scr[...] = jax.nn.sigmoid(gate)
        g = -jnp.exp(alog_ref[...]) * _softplus(gate + dtb_ref[...]) * LOG2E
        gcum = _chunk_cumsum(g)
        gcum_scr[...] = gcum
        for s in range(tb // LANES):
            gt_scr[s] = gcum[s * LANES:(s + 1) * LANES, :].T

    lane = lax.broadcasted_iota(jnp.int32, (1, LANES), 1)
    r = lax.broadcasted_iota(jnp.int32, (CHUNK, CHUNK), 0)
    c = lax.broadcasted_iota(jnp.int32, (CHUNK, CHUNK), 1)
    causal = r >= c
    strict = r > c
    rh = lax.broadcasted_iota(jnp.int32, (CHUNK, GDN_GRP * CHUNK), 0)
    ch = lax.broadcasted_iota(jnp.int32, (CHUNK, GDN_GRP * CHUNK), 1)
    eye_h = jnp.where(rh == ch % CHUNK, 1.0, 0.0)
    nw = nw_ref[...]
    heads = range(GDN_HPS)
    groups = range(GDN_NC // GDN_GRP)
    gw = GDN_GRP * CHUNK

    def l2n(x):
        return x * lax.rsqrt(jnp.sum(x * x, axis=-1, keepdims=True) + NORM_EPS)

    qa = _silu(_causal_conv(qc_ref, qp_ref, wq_ref, cq_scr, first))
    ka = _silu(_causal_conv(kc_ref, kp_ref, wk_ref, ck_scr, first))
    va = _silu(_causal_conv(vc_ref, vp_ref, wv_ref, cv_scr, first))

    gcs, ks, rhs, low_hs, p_hs, x_hs = {}, {}, {}, {}, {}, {}
    for hl in heads:
        h = hp * GDN_HPS + hl
        cols = slice(hl * LANES, (hl + 1) * LANES)
        beta = jnp.sum(jnp.where(lane == h, beta_scr[...], 0.0), axis=1, keepdims=True)
        gc = jnp.sum(jnp.where(lane == GDN_HEADS + h, gcum_scr[...], 0.0), axis=1, keepdims=True)
        q = l2n(qa[:, cols]) * (GDN_DK ** -0.5)
        k = l2n(ka[:, cols])
        eg = jnp.exp2(gc)
        kb = k * beta
        rhs[hl] = jnp.concatenate([va[:, cols] * beta, kb * eg], axis=1).astype(BF16)
        qd = (q * eg).astype(BF16)
        gcs[hl], ks[hl] = gc, k
        for grp in groups:
            lows = []
            for j in range(GDN_GRP):
                n = grp * GDN_GRP + j
                rows = slice(n * CHUNK, (n + 1) * CHUNK)
                gc_n = gc[rows]
                grow = gt_scr[n // 2, pl.ds(GDN_HEADS + h, 1), :]
                grow = grow[:, (n % 2) * CHUNK:(n % 2 + 1) * CHUNK]
                decay = jnp.where(causal, jnp.exp2(jnp.where(causal, gc_n - grow, 0.0)), 0.0)
                ab = _dot_nt(jnp.concatenate([kb[rows], q[rows]], axis=0), k[rows])
                lows.append(jnp.where(strict, ab[:CHUNK] * decay, 0.0))
                qk_scr[hl, rows, :] = (ab[CHUNK:] * decay).astype(BF16)
                wq_scr[hl, n, CHUNK:2 * CHUNK, :] = qd[rows]
            low_hs[hl, grp] = jnp.concatenate(lows, axis=1)

    for key, low_h in low_hs.items():
        xb = low_h.astype(BF16)
        p_hs[key] = eye_h - low_h
        x_hs[key] = jnp.dot(xb, _block_diag(xb), preferred_element_type=F32)
    for step in range(5):
        for key in low_hs:
            xb = x_hs[key].astype(BF16)
            x_bd = _block_diag(xb)
            p_hs[key] = p_hs[key] + jnp.dot(p_hs[key].astype(BF16), x_bd, preferred_element_type=F32)
            if step < 4:
                x_hs[key] = jnp.dot(xb, x_bd, preferred_element_type=F32)

    for (hl, grp), p_h in p_hs.items():
        g0 = grp * gw
        uw = jnp.dot(_block_diag(p_h.astype(BF16)), rhs[hl][g0:g0 + gw], preferred_element_type=F32)
        u_scr[hl, g0:g0 + gw, :] = uw[:, :GDN_DV]
        for j in range(GDN_GRP):
            n = grp * GDN_GRP + j
            rows = slice(n * CHUNK, (n + 1) * CHUNK)
            uw_n = uw[j * CHUNK:(j + 1) * CHUNK]
            wq_scr[hl, n, 0:CHUNK, :] = uw_n[:, GDN_DV:].astype(BF16)
            gc_n = gcs[hl][rows]
            kd = ks[hl][rows] * jnp.exp2(gc_n[CHUNK - 1:CHUNK, :] - gc_n)
            ba = _dot_tn(kd, uw_n)
            b_scr[hl, n] = ba[:, :GDN_DV]
            a_scr[hl, n] = ba[:, GDN_DV:].astype(BF16)

    states = [s_scr[hp * GDN_HPS + hl] for hl in heads]
    for n in range(GDN_NC):
        rows = slice(n * CHUNK, (n + 1) * CHUNK)
        for hl in heads:
            cols = slice(hl * LANES, (hl + 1) * LANES)
            sb = states[hl].astype(BF16)
            ws = jnp.dot(wq_scr[hl, n], sb, preferred_element_type=F32)
            g_last = jnp.exp2(gcs[hl][(n + 1) * CHUNK - 1:(n + 1) * CHUNK, :])
            states[hl] = states[hl] * g_last - jnp.dot(a_scr[hl, n], sb, preferred_element_type=F32) + b_scr[hl, n]
            v_new = u_scr[hl, rows, :] - ws[:CHUNK]
            o = ws[CHUNK:] + jnp.dot(qk_scr[hl, rows, :], v_new.astype(BF16), preferred_element_type=F32)
            on = o * lax.rsqrt(jnp.mean(o * o, axis=-1, keepdims=True) + NORM_EPS) * nw
            o_ref[rows, cols] = (on * _silu(z_ref[rows, cols])).astype(o_ref.dtype)
    for hl in heads:
        s_scr[hp * GDN_HPS + hl] = states[hl]


def _gated_deltanet(proj, gates, conv_w, a_log, dt_bias, norm_w):
    t = proj.shape[0]
    tb = GDN_TB
    nh = GDN_HEADS
    w = GDN_HPS * LANES
    nb = GDN_QK // w
    pad = lambda a: jnp.zeros((1, LANES), F32).at[0, nh:2 * nh].set(a.astype(F32))
    cur = lambda sec: pl.BlockSpec((tb, w), lambda i, h: (i, sec * nb + h))
    prev = lambda sec: pl.BlockSpec((HALO, w), lambda i, h: (jnp.maximum(i * (tb // HALO) - 1, 0), sec * nb + h))
    cw = lambda sec: pl.BlockSpec((CONV_K, w), lambda i, h: (0, sec * nb + h))
    row = pl.BlockSpec((1, LANES), lambda i, h: (0, 0))
    return pl.pallas_call(
        _gdn_kernel,
        grid=(t // tb, nh // GDN_HPS),
        in_specs=[cur(0), prev(0), cur(1), prev(1), cur(2), prev(2), cw(0), cw(1), cw(2),
                  pl.BlockSpec((tb, LANES), lambda i, h: (i, 0)), row, row, cur(3), row],
        out_specs=pl.BlockSpec((tb, w), lambda i, h: (i, h)),
        out_shape=jax.ShapeDtypeStruct((t, GDN_V), BF16),
        scratch_shapes=[pltpu.VMEM((tb + HALO, w), F32)] * 3 + [
            pltpu.VMEM((tb, LANES), F32), pltpu.VMEM((tb, LANES), F32),
            pltpu.VMEM((tb // LANES, LANES, LANES), F32),
            pltpu.VMEM((GDN_HPS, GDN_NC, 2 * CHUNK, GDN_DK), BF16),
            pltpu.VMEM((GDN_HPS, tb, GDN_DV), F32),
            pltpu.VMEM((GDN_HPS, tb, CHUNK), BF16),
            pltpu.VMEM((GDN_HPS, GDN_NC, GDN_DK, GDN_DK), BF16),
            pltpu.VMEM((GDN_HPS, GDN_NC, GDN_DK, GDN_DV), F32),
            pltpu.VMEM((nh, GDN_DK, GDN_DV), F32)],
        compiler_params=_params(("arbitrary", "arbitrary")),
        name="gated_deltanet",
    )(proj, proj, proj, proj, proj, proj, conv_w, conv_w, conv_w,
      gates, pad(a_log), pad(dt_bias), proj, norm_w.reshape(1, GDN_DV))


SSD_TB = 512
SSD_NC = SSD_TB // CHUNK


def _ssd_expand_matrix():
    j = jnp.arange(LANES)[:, None]
    col = jnp.arange(2 * SSD_GW)[None, :]
    head = (col % SSD_GW) // SSD_HEADDIM
    piece = j // SSD_HPG
    is_ac = (col < SSD_GW) & (piece < 3)
    is_dt = (col >= SSD_GW) & (piece >= 3) & (piece < 5)
    return jnp.where((is_ac | is_dt) & (j % SSD_HPG == head), 1.0, 0.0).astype(BF16)


def _ssd_kernel(xc_ref, xp_ref, bc_ref, bp_ref, cc_ref, cp_ref, wx_ref, wb_ref, wc_ref,
                bx_ref, bb_ref, bcn_ref, dt_ref, dtb_ref, alog_ref, dsk_ref, z_ref, nw_ref, ex_ref, o_ref,
                cx_scr, cb_scr, cc_scr, dt_scr, ac_scr, x_scr, e_scr, st_scr):
    i = pl.program_id(0)
    g = pl.program_id(1)
    tb = SSD_TB
    hpg = SSD_HPG
    first = i == 0

    @pl.when(first)
    def _():
        st_scr[g] = jnp.zeros(st_scr.shape[1:], F32)

    @pl.when(g == 0)
    def _():
        dt = _softplus(dt_ref[...] + dtb_ref[...])
        dt_scr[...] = dt
        ac_scr[...] = _chunk_cumsum(dt * (-jnp.exp(alog_ref[...]) * LOG2E))

    shift = (LANES - g * hpg) % LANES
    lane = lax.broadcasted_iota(jnp.int32, (1, LANES), 1)
    mine = lane < hpg
    dt_g = jnp.where(mine, pltpu.roll(dt_scr[...], shift, 1), 0.0)
    ac_g = jnp.where(mine, pltpu.roll(ac_scr[...], shift, 1), 0.0)

    a1, a2, a3 = _split3(ac_g)
    d1 = dt_g.astype(BF16)
    d2 = (dt_g - d1.astype(F32)).astype(BF16)
    packed = a1.astype(F32)
    for k, piece in enumerate((a2, a3, d1, d2), start=1):
        packed = packed + pltpu.roll(piece.astype(F32), k * hpg, 1)
    e_scr[...] = jnp.dot(packed.astype(BF16), ex_ref[...], preferred_element_type=F32)

    ac_t = [ac_g[s * LANES:(s + 1) * LANES, :].T[0:8, :] for s in range(tb // LANES)]
    ac_tr = [pltpu.roll(a, CHUNK, 1) for a in ac_t]

    x_scr[...] = _silu(_causal_conv(xc_ref, xp_ref, wx_ref, cx_scr, first) + bx_ref[...])
    bm = _silu(_causal_conv(bc_ref, bp_ref, wb_ref, cb_scr, first) + bb_ref[...])
    cm = _silu(_causal_conv(cc_ref, cp_ref, wc_ref, cc_scr, first) + bcn_ref[...])

    r = lax.broadcasted_iota(jnp.int32, (CHUNK, LANES), 0)
    c = lax.broadcasted_iota(jnp.int32, (CHUNK, LANES), 1)
    causal2 = r >= (c % CHUNK)
    left = c < CHUNK
    dsk = dsk_ref[...]
    nw = nw_ref[...]
    gw = SSD_GW

    chunks = range(SSD_NC)
    rows = [slice(n * CHUNK, (n + 1) * CHUNK) for n in chunks]
    cbs = [_dot_nt(cm[rows[n]], bm[rows[n]]) for n in chunks]
    y_diag, upd, dec = [], [], []
    for n in chunks:
        acx = e_scr[rows[n], 0:gw]
        xdt = x_scr[rows[n], :] * e_scr[rows[n], gw:2 * gw]
        a_last = acx[CHUNK - 1:CHUNK, :]
        cb2 = jnp.concatenate([cbs[n], cbs[n]], axis=1)
        t_lo, t_hi = (ac_t[n // 2], ac_tr[n // 2]) if n % 2 == 0 else (ac_tr[n // 2], ac_t[n // 2])
        ys = []
        for p in range(hpg // 2):
            lanes = slice(p * LANES, (p + 1) * LANES)
            a_row = jnp.where(lane < CHUNK, t_lo[2 * p:2 * p + 1, :], t_hi[2 * p + 1:2 * p + 2, :])
            decay = jnp.where(causal2, jnp.exp2(jnp.where(causal2, acx[:, lanes] - a_row, 0.0)), 0.0)
            xp = xdt[:, lanes]
            x_bd = jnp.concatenate([jnp.where(left, xp, 0.0), jnp.where(left, 0.0, xp)], axis=0)
            ys.append(_dot(cb2 * decay, x_bd))
        y_diag.append(jnp.concatenate(ys, axis=1))
        upd.append(_dot_tn(bm[rows[n]], xdt * jnp.exp2(a_last - acx)))
        dec.append(jnp.exp2(a_last))

    state = st_scr[g]
    states = []
    for n in chunks:
        states.append(state.astype(BF16))
        state = state * dec[n] + upd[n]
    st_scr[g] = state

    for n in chunks:
        acx = e_scr[rows[n], 0:gw]
        y_off = jnp.dot(cm[rows[n]].astype(BF16), states[n], preferred_element_type=F32)
        y = y_diag[n] + y_off * jnp.exp2(acx) + x_scr[rows[n], :] * dsk
        y = y * _silu(z_ref[rows[n], :])
        y = y * lax.rsqrt(jnp.mean(y * y, axis=-1, keepdims=True) + NORM_EPS) * nw
        o_ref[rows[n], :] = y.astype(o_ref.dtype)


def _mamba2_ssd(proj, dt_raw, conv_w, conv_b, dt_bias, a_log, d_skip, norm_w):
    t = proj.shape[0]
    tb = SSD_TB
    gw = SSD_GW
    ng = SSD_GROUPS
    x0 = SSD_DINNER // gw
    b0 = 2 * SSD_DINNER // LANES
    c0 = b0 + ng
    pad = lambda a: jnp.zeros((1, LANES), F32).at[0, :SSD_HEADS].set(a.astype(F32))
    prev_idx = lambda i: jnp.maximum(i * (tb // HALO) - 1, 0)
    cbias = conv_b.reshape(1, -1)
    dsk = jnp.repeat(d_skip.astype(F32), SSD_HEADDIM).reshape(1, SSD_DINNER)
    return pl.pallas_call(
        _ssd_kernel,
        grid=(t // tb, ng),
        in_specs=[pl.BlockSpec((tb, gw), lambda i, g: (i, x0 + g)),
                  pl.BlockSpec((HALO, gw), lambda i, g: (prev_idx(i), x0 + g)),
                  pl.BlockSpec((tb, LANES), lambda i, g: (i, b0 + g)),
                  pl.BlockSpec((HALO, LANES), lambda i, g: (prev_idx(i), b0 + g)),
                  pl.BlockSpec((tb, LANES), lambda i, g: (i, c0 + g)),
                  pl.BlockSpec((HALO, LANES), lambda i, g: (prev_idx(i), c0 + g)),
                  pl.BlockSpec((CONV_K, gw), lambda i, g: (0, g)),
                  pl.BlockSpec((CONV_K, LANES), lambda i, g: (0, SSD_DINNER // LANES + g)),
                  pl.BlockSpec((CONV_K, LANES), lambda i, g: (0, SSD_DINNER // LANES + ng + g)),
                  pl.BlockSpec((1, gw), lambda i, g: (0, g)),
                  pl.BlockSpec((1, LANES), lambda i, g: (0, SSD_DINNER // LANES + g)),
                  pl.BlockSpec((1, LANES), lambda i, g: (0, SSD_DINNER // LANES + ng + g)),
                  pl.BlockSpec((tb, LANES), lambda i, g: (i, 0)),
                  pl.BlockSpec((1, LANES), lambda i, g: (0, 0)),
                  pl.BlockSpec((1, LANES), lambda i, g: (0, 0)),
                  pl.BlockSpec((1, gw), lambda i, g: (0, g)),
                  pl.BlockSpec((tb, gw), lambda i, g: (i, g)),
                  pl.BlockSpec((1, gw), lambda i, g: (0, g)),
                  pl.BlockSpec((LANES, 2 * gw), lambda i, g: (0, 0))],
        out_specs=pl.BlockSpec((tb, gw), lambda i, g: (i, g)),
        out_shape=jax.ShapeDtypeStruct((t, SSD_DINNER), BF16),
        scratch_shapes=[pltpu.VMEM((tb + HALO, gw), F32), pltpu.VMEM((tb + HALO, LANES), F32),
                        pltpu.VMEM((tb + HALO, LANES), F32),
                        pltpu.VMEM((tb, LANES), F32), pltpu.VMEM((tb, LANES), F32),
                        pltpu.VMEM((tb, gw), F32), pltpu.VMEM((tb, 2 * gw), F32),
                        pltpu.VMEM((ng, SSD_DSTATE, gw), F32)],
        compiler_params=_params(("arbitrary", "arbitrary")),
        name="mamba2_ssd",
    )(proj, proj, proj, proj, proj, proj, conv_w, conv_w, conv_w, cbias, cbias, cbias,
      dt_raw, pad(dt_bias), pad(a_log), dsk, proj, norm_w.reshape(1, SSD_DINNER), _ssd_expand_matrix())


def _layer_mods(mod, layer):
    d = D_MODEL
    return [mod[layer, k * d:(k + 1) * d].reshape(1, d) for k in range(6)]


def kernel(x, c, mod_w, mod_b, norm_mix_w, norm_mlp_w, mlp_w1, mlp_w2, ab_w_in, gdn_conv_w, gdn_a_log,
           gdn_dt_bias, gdn_norm_w, attn_q_norm_w, attn_k_norm_w, attn_rel_bias, ab_w_out, ssd_w_in,
           ssd_conv_w, ssd_conv_b, ssd_dt_bias, ssd_a_log, ssd_d, ssd_norm_w, ssd_w_out):
    b, t, d = x.shape
    assert b == 1 and d == D_MODEL
    xs = x.reshape(t, d)
    mod = _modulation(c, mod_w, mod_b)
    w1_b, w2_b = mlp_w1.astype(BF16), mlp_w2.astype(BF16)

    sh1, sc1, g1, sh2, sc2, g2 = _layer_mods(mod, 0)
    gate_lo = 2 * GDN_QK + 2 * GDN_V
    gate_hi = gate_lo + 2 * GDN_HEADS
    w_in_b, w_gate = _cast_weight(ab_w_in, gate_lo)
    w_gate = jnp.pad(w_gate[:, :2 * GDN_HEADS], ((0, 0), (0, LANES - 2 * GDN_HEADS)))
    proj, gates = _norm_proj2(xs, norm_mix_w[0].reshape(1, d), sc1, sh1, w_in_b, gate_lo, w_in_b[:, gate_hi:ab_w_in.shape[2]],
                              w_gate, tm=256)
    o_a = _gated_deltanet(proj, gates, gdn_conv_w[0], gdn_a_log[0], gdn_dt_bias[0], gdn_norm_w[0])
    a0 = gate_lo // ATT_W
    o_b = _band_attention(proj, a0, a0 + 1, a0 + 2, attn_q_norm_w[0], attn_k_norm_w[0], attn_rel_bias[0])
    xs, h2 = _out_proj([o_a, o_b], [ab_w_out[0][:GDN_V], ab_w_out[0][GDN_V:]], xs, g1,
                       norm_mlp_w[0].reshape(1, d), sc2, sh2)
    xs = _mlp(xs, h2, g2, w1_b, w2_b, 0)

    sh1, sc1, g1, sh2, sc2, g2 = _layer_mods(mod, 1)
    n_main = 2 * SSD_DINNER + 2 * SSD_GROUPS * SSD_DSTATE
    w_in_b, w_dt = _cast_weight(ssd_w_in, n_main)
    w_dt = jnp.pad(w_dt[:, :SSD_HEADS], ((0, 0), (0, LANES - SSD_HEADS)))
    proj, dt_raw = _norm_proj(xs, norm_mix_w[1].reshape(1, d), sc1, sh1, w_in_b, n_main, w_dt, tm=512)
    y = _mamba2_ssd(proj, dt_raw, ssd_conv_w[0], ssd_conv_b[0], ssd_dt_bias[0], ssd_a_log[0],
                    ssd_d[0], ssd_norm_w[0])
    xs, h2 = _out_proj([y], [ssd_w_out[0]], xs, g1, norm_mlp_w[1].reshape(1, d), sc2, sh2)
    xs = _mlp(xs, h2, g2, w1_b, w2_b, 1)
    return xs.reshape(b, t, d)
```
